```python
import jax, jax.numpy as jnp
from jax import lax
import numpy as np

D_MODEL = 1024
BATCH = 1
SEQ = 16384
DEPTH = 4
DEC_BATCH = 8
DEC_SEQ = 16
PAST_LEN = 1024

CHUNK = 64
HEAD_DIM = 64
A_HEADS = 8
A_KV_HEADS = 2
A_GROUP = A_HEADS // A_KV_HEADS
A_WINDOW = 128
A_BACK = A_WINDOW // CHUNK
B_HEADS = 8
B_BACK = 8
REL_CLIP = 256
X_HEADS = 4
X_HEAD_DIM = 128
N_MEM = 256
D_FF = 2816
CONV_W = 3
EPS = 1e-6
NEG = -1e30
A_Q = A_HEADS * HEAD_DIM
A_KV = A_KV_HEADS * HEAD_DIM
B_W = B_HEADS * HEAD_DIM
X_W = X_HEADS * X_HEAD_DIM
D_IN = A_Q + 2 * A_KV + 3 * B_W + 2 * D_MODEL

kernel_name = 'hybrid_streaming_swa_chunkband_step'


def rmsnorm(x, g):
    xf = x.astype(jnp.float32)
    y = xf * lax.rsqrt(jnp.mean(xf * xf, axis=-1, keepdims=True) + EPS)
    return (y * g.astype(jnp.float32)).astype(x.dtype)


def alibi_slopes():
    return jnp.asarray(np.exp2(-8.0 * np.arange(1, A_HEADS + 1) / A_HEADS), dtype=jnp.float32)


def alibi_bias(slopes, pos_q, pos_k):
    dist = jnp.abs(pos_q[:, None] - pos_k[None, :]).astype(jnp.float32)
    return -slopes.reshape(A_KV_HEADS, A_GROUP, 1, 1) * dist


def rel_pos_bias(table, pos_q, pos_k):
    idx = jnp.clip(pos_k[None, :] - pos_q[:, None], -REL_CLIP, REL_CLIP) + REL_CLIP
    return table.astype(jnp.float32)[:, idx][:, None]


def band_positions(back):
    pos_q = back * CHUNK + jnp.arange(CHUNK)
    pos_k = jnp.arange((back + 1) * CHUNK)
    return pos_q, pos_k


def band_valid_add(nc, back):
    kc = jnp.arange(nc)[:, None] - back + jnp.arange((back + 1) * CHUNK)[None, :] // CHUNK
    return jnp.where(kc >= 0, 0.0, NEG)[:, None, None, None, :]


def band_mask_add(pos_q, pos_k, back):
    cq = pos_q[:, None] // CHUNK
    ck = pos_k[None, :] // CHUNK
    return jnp.where((ck <= cq) & (ck >= cq - back), 0.0, NEG)


def chunk_band(t, back):
    nc = t.shape[1]
    pad = jnp.pad(t, [(0, 0), (back, 0)] + [(0, 0)] * (t.ndim - 2))
    band = jnp.stack([pad[:, j:j + nc] for j in range(back + 1)], axis=2)
    return band.reshape(t.shape[:2] + ((back + 1) * t.shape[2],) + t.shape[3:])


def attend(q, k, v, bias, mask_add, sink=None):
    s = jnp.einsum('...qkgd,...skd->...kgqs', q.astype(jnp.float32), k.astype(jnp.float32)) * (q.shape[-1] ** -0.5)
    if bias is not None:
        s = s + bias
    if mask_add is not None:
        s = s + mask_add
    if sink is not None:
        sk = jnp.broadcast_to(sink.astype(jnp.float32)[:, :, None, None], s.shape[:-1] + (1,))
        p = jax.nn.softmax(jnp.concatenate([s, sk], axis=-1), axis=-1)[..., :-1]
    else:
        p = jax.nn.softmax(s, axis=-1)
    return jnp.einsum('...kgqs,...skd->...qkgd', p.astype(v.dtype), v)


def project_in(h, w_in):
    z = h @ w_in
    c0 = A_Q
    c1 = c0 + A_KV
    c2 = c1 + A_KV
    c3 = c2 + B_W
    c4 = c3 + B_W
    c5 = c4 + B_W
    c6 = c5 + D_MODEL
    return jnp.split(z, [c0, c1, c2, c3, c4, c5, c6], axis=-1)


def merge_branches(oa, ob, ga, gb, w_o_a, w_o_b, w_out):
    m = jax.nn.sigmoid(ga) * (oa @ w_o_a) + jax.nn.sigmoid(gb) * (ob @ w_o_b)
    return m @ w_out


def mixer_prompt(h, slopes, w_in, sink, rel_table, w_o_a, w_o_b, w_out):
    bp, t, _ = h.shape
    nc = t // CHUNK
    qa, ka, va, qb, kb, vb, ga, gb = project_in(h, w_in)
    ka = ka.reshape(bp, t, A_KV_HEADS, HEAD_DIM)
    va = va.reshape(bp, t, A_KV_HEADS, HEAD_DIM)
    kb = kb.reshape(bp, t, B_HEADS, HEAD_DIM)
    vb = vb.reshape(bp, t, B_HEADS, HEAD_DIM)
    qa_c = qa.reshape(bp, nc, CHUNK, A_KV_HEADS, A_GROUP, HEAD_DIM)
    ka_b = chunk_band(ka.reshape(bp, nc, CHUNK, A_KV_HEADS, HEAD_DIM), A_BACK)
    va_b = chunk_band(va.reshape(bp, nc, CHUNK, A_KV_HEADS, HEAD_DIM), A_BACK)
    pq, pk = band_positions(A_BACK)
    oa = attend(qa_c, ka_b, va_b, alibi_bias(slopes, pq, pk), band_valid_add(nc, A_BACK),
                sink.reshape(A_KV_HEADS, A_GROUP)).reshape(bp, t, A_Q)
    qb_c = qb.reshape(bp, nc, CHUNK, B_HEADS, 1, HEAD_DIM)
    kb_b = chunk_band(kb.reshape(bp, nc, CHUNK, B_HEADS, HEAD_DIM), B_BACK)
    vb_b = chunk_band(vb.reshape(bp, nc, CHUNK, B_HEADS, HEAD_DIM), B_BACK)
    pq, pk = band_positions(B_BACK)
    ob = attend(qb_c, kb_b, vb_b, rel_pos_bias(rel_table, pq, pk), band_valid_add(nc, B_BACK)).reshape(bp, t, B_W)
    y = merge_branches(oa, ob, ga, gb, w_o_a, w_o_b, w_out)
    na = min(A_WINDOW, t)
    nb = min(B_BACK * CHUNK, t)
    return y, ka[:, t - na:], va[:, t - na:], kb[:, t - nb:], vb[:, t - nb:]


def mixer_sample(h, cak, cav, cbk, cbv, slopes, w_in, sink, rel_table, w_o_a, w_o_b, w_out):
    bd, tn, _ = h.shape
    qa, ka, va, qb, kb, vb, ga, gb = project_in(h, w_in)
    ka = ka.reshape(bd, tn, A_KV_HEADS, HEAD_DIM)
    va = va.reshape(bd, tn, A_KV_HEADS, HEAD_DIM)
    kb = kb.reshape(bd, tn, B_HEADS, HEAD_DIM)
    vb = vb.reshape(bd, tn, B_HEADS, HEAD_DIM)
    pos_q = PAST_LEN + jnp.arange(tn)
    la = cak.shape[1]
    pk_a = jnp.concatenate([PAST_LEN - la + jnp.arange(la), pos_q])
    oa = attend(qa.reshape(bd, tn, A_KV_HEADS, A_GROUP, HEAD_DIM),
                jnp.concatenate([cak.astype(ka.dtype), ka], axis=1),
                jnp.concatenate([cav.astype(va.dtype), va], axis=1),
                alibi_bias(slopes, pos_q, pk_a), band_mask_add(pos_q, pk_a, A_BACK),
                sink.reshape(A_KV_HEADS, A_GROUP)).reshape(bd, tn, A_Q)
    lb = cbk.shape[1]
    pk_b = jnp.concatenate([PAST_LEN - lb + jnp.arange(lb), pos_q])
    ob = attend(qb.reshape(bd, tn, B_HEADS, 1, HEAD_DIM),
                jnp.concatenate([cbk.astype(kb.dtype), kb], axis=1),
                jnp.concatenate([cbv.astype(vb.dtype), vb], axis=1),
                rel_pos_bias(rel_table, pos_q, pk_b), band_mask_add(pos_q, pk_b, B_BACK)).reshape(bd, tn, B_W)
    y = merge_branches(oa, ob, ga, gb, w_o_a, w_o_b, w_out)
    return y, ka, va, kb, vb


def mem_kv(mem, g, w_k, w_v):
    b, m, _ = mem.shape
    hm = rmsnorm(mem, g)
    return ((hm @ w_k).reshape(b, m, X_HEADS, X_HEAD_DIM), (hm @ w_v).reshape(b, m, X_HEADS, X_HEAD_DIM))


def cross_attn(h, mk, mv, w_q, w_o):
    b, t, _ = h.shape
    q = (h @ w_q).reshape(b, t, X_HEADS, 1, X_HEAD_DIM)
    o = attend(q, mk.astype(h.dtype), mv.astype(h.dtype), None, None)
    return o.reshape(b, t, X_W) @ w_o


def conv_ffn(h, conv_state, w_up, w_conv, b_conv, w_down):
    u = h @ w_up
    t = u.shape[1]
    ext = jnp.concatenate([conv_state.astype(u.dtype), u], axis=1)
    c = b_conv
    for j in range(CONV_W):
        c = c + ext[:, j:j + t] * w_conv[j]
    gate, up = jnp.split(c, 2, axis=-1)
    return (jax.nn.silu(gate) * up) @ w_down, ext[:, t:]


def setup_inputs(seed: int = 0) -> dict:
    key = jax.random.key(seed)
    ks = jax.random.split(key, 32)

    def nrm(k, shape, scale=1.0):
        return scale * jax.random.normal(k, shape, jnp.float32)

    la = min(A_WINDOW, PAST_LEN)
    lb = min(B_BACK * CHUNK, PAST_LEN)
    return {
        'x_prompt': nrm(ks[0], (BATCH, SEQ, D_MODEL)),
        'x_sample': nrm(ks[1], (DEC_BATCH, DEC_SEQ, D_MODEL)),
        'mem_prompt': nrm(ks[2], (BATCH, N_MEM, D_MODEL)),
        'cache_a_k': nrm(ks[3], (DEPTH, DEC_BATCH, la, A_KV_HEADS, HEAD_DIM)),
        'cache_a_v': nrm(ks[4], (DEPTH, DEC_BATCH, la, A_KV_HEADS, HEAD_DIM)),
        'cache_b_k': nrm(ks[5], (DEPTH, DEC_BATCH, lb, B_HEADS, HEAD_DIM)),
        'cache_b_v': nrm(ks[6], (DEPTH, DEC_BATCH, lb, B_HEADS, HEAD_DIM)),
        'cache_mem_k': nrm(ks[7], (DEPTH, DEC_BATCH, N_MEM, X_HEADS, X_HEAD_DIM)),
        'cache_mem_v': nrm(ks[8], (DEPTH, DEC_BATCH, N_MEM, X_HEADS, X_HEAD_DIM)),
        'state_conv': nrm(ks[9], (DEPTH, DEC_BATCH, CONV_W - 1, 2 * D_FF)),
        'g_mix': 1.0 + nrm(ks[10], (DEPTH, D_MODEL), 0.02),
        'w_mix_in': nrm(ks[11], (DEPTH, D_MODEL, D_IN), D_MODEL ** -0.5),
        'a_sink': nrm(ks[12], (DEPTH, A_HEADS), 0.5),
        'b_rel_bias': nrm(ks[13], (DEPTH, B_HEADS, 2 * REL_CLIP + 1), 0.1),
        'w_o_a': nrm(ks[14], (DEPTH, A_Q, D_MODEL), A_Q ** -0.5),
        'w_o_b': nrm(ks[15], (DEPTH, B_W, D_MODEL), B_W ** -0.5),
        'w_mix_out': nrm(ks[16], (DEPTH, D_MODEL, D_MODEL), D_MODEL ** -0.5),
        'g_xattn': 1.0 + nrm(ks[17], (DEPTH, D_MODEL), 0.02),
        'g_mem': 1.0 + nrm(ks[18], (DEPTH, D_MODEL), 0.02),
        'w_xq': nrm(ks[19], (DEPTH, D_MODEL, X_W), D_MODEL ** -0.5),
        'w_xk': nrm(ks[20], (DEPTH, D_MODEL, X_W), D_MODEL ** -0.5),
        'w_xv': nrm(ks[21], (DEPTH, D_MODEL, X_W), D_MODEL ** -0.5),
        'w_xo': nrm(ks[22], (DEPTH, X_W, D_MODEL), X_W ** -0.5),
        'g_ffn': 1.0 + nrm(ks[23], (DEPTH, D_MODEL), 0.02),
        'w_up': nrm(ks[24], (DEPTH, D_MODEL, 2 * D_FF), D_MODEL ** -0.5),
        'w_conv': nrm(ks[25], (DEPTH, CONV_W, 2 * D_FF), CONV_W ** -0.5),
        'b_conv': nrm(ks[26], (DEPTH, 2 * D_FF), 0.01),
        'w_down': nrm(ks[27], (DEPTH, D_FF, D_MODEL), D_FF ** -0.5),
        'g_final': 1.0 + nrm(ks[28], (D_MODEL,), 0.02),
    }


def reference(x_prompt, x_sample, mem_prompt, cache_a_k, cache_a_v, cache_b_k, cache_b_v,
              cache_mem_k, cache_mem_v, state_conv, g_mix, w_mix_in, a_sink, b_rel_bias,
              w_o_a, w_o_b, w_mix_out, g_xattn, g_mem, w_xq, w_xk, w_xv, w_xo,
              g_ffn, w_up, w_conv, b_conv, w_down, g_final):
    slopes = alibi_slopes()
    xp = x_prompt
    xs = x_sample
    l_pa_k, l_pa_v, l_pb_k, l_pb_v, l_pm_k, l_pm_v, l_pconv = [], [], [], [], [], [], []
    l_sa_k, l_sa_v, l_sb_k, l_sb_v, l_sconv = [], [], [], [], []
    for l in range(DEPTH):
        wts = (w_mix_in[l], a_sink[l], b_rel_bias[l], w_o_a[l], w_o_b[l], w_mix_out[l])
        y, ak, av, bk, bv = mixer_prompt(rmsnorm(xp, g_mix[l]), slopes, *wts)
        xp = xp + y
        mk, mv = mem_kv(mem_prompt, g_mem[l], w_xk[l], w_xv[l])
        xp = xp + cross_attn(rmsnorm(xp, g_xattn[l]), mk, mv, w_xq[l], w_xo[l])
        zero_state = jnp.zeros((xp.shape[0], CONV_W - 1, 2 * D_FF), xp.dtype)
        y, cp = conv_ffn(rmsnorm(xp, g_ffn[l]), zero_state, w_up[l], w_conv[l], b_conv[l], w_down[l])
        xp = xp + y
        l_pa_k.append(ak); l_pa_v.append(av); l_pb_k.append(bk); l_pb_v.append(bv)
        l_pm_k.append(mk); l_pm_v.append(mv); l_pconv.append(cp)
        y, ak, av, bk, bv = mixer_sample(rmsnorm(xs, g_mix[l]), cache_a_k[l], cache_a_v[l],
                                         cache_b_k[l], cache_b_v[l], slopes, *wts)
        xs = xs + y
        xs = xs + cross_attn(rmsnorm(xs, g_xattn[l]), cache_mem_k[l], cache_mem_v[l], w_xq[l], w_xo[l])
        y, cs = conv_ffn(rmsnorm(xs, g_ffn[l]), state_conv[l], w_up[l], w_conv[l], b_conv[l], w_down[l])
        xs = xs + y
        l_sa_k.append(ak); l_sa_v.append(av); l_sb_k.append(bk); l_sb_v.append(bv); l_sconv.append(cs)
    y_prompt = rmsnorm(xp, g_final)
    y_sample = rmsnorm(xs, g_final)
    pa_k = jnp.stack(l_pa_k)
    pa_v = jnp.stack(l_pa_v)
    pb_k = jnp.stack(l_pb_k)
    pb_v = jnp.stack(l_pb_v)
    pm_k = jnp.stack(l_pm_k)
    pm_v = jnp.stack(l_pm_v)
    pconv = jnp.stack(l_pconv)
    sa_k = jnp.stack(l_sa_k)
    sa_v = jnp.stack(l_sa_v)
    sb_k = jnp.stack(l_sb_k)
    sb_v = jnp.stack(l_sb_v)
    sconv = jnp.stack(l_sconv)
    return (y_prompt, y_sample, pa_k, pa_v, pb_k, pb_v, pm_k, pm_v, pconv, sa_k, sa_v, sb_k, sb_v, sconv)
```

```python
import functools

import numpy as np
import jax
import jax.numpy as jnp
from jax import lax
from jax.experimental import pallas as pl
from jax.experimental.pallas import tpu as pltpu

CHUNK = 64
HEAD_DIM = 64
A_HEADS = 8
A_KV_HEADS = 2
A_GROUP = A_HEADS // A_KV_HEADS
A_BACK = 2
B_HEADS = 8
B_BACK = 8
REL_CLIP = 256
X_HEADS = 4
X_HEAD_DIM = 128
EPS = 1e-6
NEG = -1e30

A_Q = A_HEADS * HEAD_DIM
A_KV = A_KV_HEADS * HEAD_DIM
B_W = B_HEADS * HEAD_DIM
X_W = X_HEADS * X_HEAD_DIM
CONV_W = 3
PAST_LEN = 1024

PAIR = 2 * CHUNK
ROW_TILE = 512
A_BAND = (A_BACK + 2) * CHUNK
B_BAND = (B_BACK + 2) * CHUNK
FF_CHUNK = 256

V7X_VMEM_LIMIT = 56 * 1024 * 1024

F32 = jnp.float32
BF16 = jnp.bfloat16


def _params(n_axes=1):
    return pltpu.CompilerParams(
        dimension_semantics=("arbitrary",) * n_axes,
        vmem_limit_bytes=V7X_VMEM_LIMIT)


def _resident(shape, index_map):
    return pl.BlockSpec(shape, index_map, pipeline_mode=pl.Buffered(1))


def _rmsnorm(x, g):
    ms = jnp.mean(x * x, axis=-1, keepdims=True)
    return (x * lax.rsqrt(ms + EPS)) * g


def _dot(a, b):
    return jnp.dot(a, b, preferred_element_type=F32)


def _dot_nt(a, b):
    return lax.dot_general(a, b, (((1,), (1,)), ((), ())), preferred_element_type=F32)


def _softmax_pv(s_list, v_list, sink=None):
    m = s_list[0].max(axis=-1, keepdims=True)
    for s in s_list[1:]:
        m = jnp.maximum(m, s.max(axis=-1, keepdims=True))
    if sink is not None:
        m = jnp.maximum(m, sink)
    l = None
    o = None
    for s, v in zip(s_list, v_list):
        p = jnp.exp(s - m)
        ls = p.sum(axis=-1, keepdims=True)
        os_ = _dot(p.astype(BF16), v)
        l = ls if l is None else l + ls
        o = os_ if o is None else o + os_
    if sink is not None:
        l = l + jnp.exp(sink - m)
    return o * (1.0 / l)


def _lane_lo(shape):
    return lax.broadcasted_iota(jnp.int32, shape, len(shape) - 1) < HEAD_DIM


def _memkv_kernel(mem_ref, g_ref, wk_ref, wv_ref, k_ref, v_ref):
    hm = _rmsnorm(mem_ref[...], g_ref[...]).astype(BF16)
    k_ref[...] = _dot(hm, wk_ref[...])
    v_ref[...] = _dot(hm, wv_ref[...])


def _memkv(mem, g_mem, w_xk, w_xv):
    depth, d, xw = w_xk.shape
    n_mem = mem.shape[0]
    return pl.pallas_call(
        _memkv_kernel,
        grid=(depth,),
        in_specs=[
            pl.BlockSpec((n_mem, d), lambda l: (0, 0)),
            pl.BlockSpec((None, 1, d), lambda l: (l, 0, 0)),
            pl.BlockSpec((None, d, xw), lambda l: (l, 0, 0)),
            pl.BlockSpec((None, d, xw), lambda l: (l, 0, 0)),
        ],
        out_specs=[
            pl.BlockSpec((None, n_mem, xw), lambda l: (l, 0, 0)),
            pl.BlockSpec((None, n_mem, xw), lambda l: (l, 0, 0)),
        ],
        out_shape=[jax.ShapeDtypeStruct((depth, n_mem, xw), F32)] * 2,
        compiler_params=_params(),
        name="mem_kv",
    )(mem, g_mem.reshape(depth, 1, d), w_xk, w_xv)


N_GATE = 2048
N_QKV = A_Q + 3 * B_W + 2 * A_KV
N_TAIL = 2 * B_W + 2 * A_KV


def _inproj_kernel(x_ref, g_ref, w_ref, qkv_ref, gate_ref, tail_ref):
    h = _rmsnorm(x_ref[...], g_ref[...]).astype(BF16)
    gate_ref[...] = _dot(h, w_ref[:, :N_GATE])
    z = _dot(h, w_ref[:, N_GATE:])
    qkv_ref[...] = z.astype(BF16)

    @pl.when(pl.program_id(0) == pl.num_programs(0) - 1)
    def _():
        tail_ref[...] = z[:, N_QKV - N_TAIL:]


def _inproj(x, g, w, tm):
    t, d = x.shape
    n = w.shape[1]
    return pl.pallas_call(
        _inproj_kernel,
        grid=(t // tm,),
        in_specs=[
            pl.BlockSpec((tm, d), lambda i: (i, 0)),
            _resident((1, d), lambda i: (0, 0)),
            _resident((d, n), lambda i: (0, 0)),
        ],
        out_specs=[
            pl.BlockSpec((tm, N_QKV), lambda i: (i, 0)),
            pl.BlockSpec((tm, N_GATE), lambda i: (i, 0)),
            pl.BlockSpec((tm, N_TAIL), lambda i: (0, 0)),
        ],
        out_shape=[
            jax.ShapeDtypeStruct((t, N_QKV), BF16),
            jax.ShapeDtypeStruct((t, N_GATE), F32),
            jax.ShapeDtypeStruct((tm, N_TAIL), F32),
        ],
        compiler_params=_params(),
        name="inproj",
    )(x, g.reshape(1, d), w)


def _attn_prompt_kernel(qa_ref, qb_ref, kbc_ref, kbp_ref, vbc_ref, vbp_ref,
                        kvac_ref, kvap_ref, bias_a_ref, bias_b_ref, sink_ref,
                        mask_a_ref, mask_b_ref, oa_ref, ob_ref,
                        kcat, vcat, kvacat):
    tm = qa_ref.shape[0]
    kcat[0:tm, :] = kbp_ref[...]
    kcat[tm:2 * tm, :] = kbc_ref[...]
    vcat[0:tm, :] = vbp_ref[...]
    vcat[tm:2 * tm, :] = vbc_ref[...]
    kvacat[0:PAIR, :] = kvap_ref[...]
    kvacat[PAIR:PAIR + tm, :] = kvac_ref[...]
    lo = _lane_lo((PAIR, 2 * HEAD_DIM))
    lo_a = _lane_lo((A_GROUP * PAIR, 2 * HEAD_DIM))

    def pair_body(d, carry):
        r0 = pl.multiple_of(d * PAIR, PAIR)
        ka2 = kvacat[pl.ds(r0, A_BAND), 0:A_KV]
        va2 = kvacat[pl.ds(r0, A_BAND), A_KV:2 * A_KV]
        qa = qa_ref[pl.ds(r0, PAIR), :]
        q_stack = jnp.concatenate(
            [qa[:, p * 2 * HEAD_DIM:(p + 1) * 2 * HEAD_DIM] for p in range(A_GROUP)], axis=0)
        mask_a = mask_a_ref[d]
        o_kv = []
        for kv in range(A_KV_HEADS):
            qm = jnp.where(lo_a, q_stack, 0) if kv == 0 else jnp.where(lo_a, 0, q_stack)
            s = _dot_nt(qm, ka2) + bias_a_ref[kv] + mask_a
            o_kv.append(_softmax_pv([s], [va2], sink_ref[kv]))
        for p in range(A_GROUP):
            o_pair = jnp.where(lo, o_kv[0][p * PAIR:(p + 1) * PAIR], o_kv[1][p * PAIR:(p + 1) * PAIR])
            oa_ref[pl.ds(r0, PAIR), p * 2 * HEAD_DIM:(p + 1) * 2 * HEAD_DIM] = o_pair.astype(BF16)
        mask_b = mask_b_ref[d]
        for hp in range(B_HEADS // 2):
            c0 = hp * 2 * HEAD_DIM
            q2 = qb_ref[pl.ds(r0, PAIR), c0:c0 + 2 * HEAD_DIM]
            k2 = kcat[pl.ds(r0, B_BAND), c0:c0 + 2 * HEAD_DIM]
            v2 = vcat[pl.ds(r0, B_BAND), c0:c0 + 2 * HEAD_DIM]
            o_h = []
            for half in range(2):
                qm = jnp.where(lo, q2, 0) if half == 0 else jnp.where(lo, 0, q2)
                s = _dot_nt(qm, k2) + bias_b_ref[2 * hp + half] + mask_b
                o_h.append(_softmax_pv([s], [v2]))
            ob_ref[pl.ds(r0, PAIR), c0:c0 + 2 * HEAD_DIM] = jnp.where(lo, o_h[0], o_h[1]).astype(BF16)
        return carry

    lax.fori_loop(0, tm // PAIR, pair_body, 0)


def _attn_prompt(qkv, bias_a, bias_b, sink_rows, mask_a, mask_b):
    t = qkv.shape[0]
    tm = ROW_TILE
    npair = tm // PAIR
    prev = lambda i: jnp.maximum(i - 1, 0)
    return pl.pallas_call(
        _attn_prompt_kernel,
        grid=(t // tm,),
        in_specs=[
            pl.BlockSpec((tm, A_Q), lambda i: (i, 0)),
            pl.BlockSpec((tm, B_W), lambda i: (i, 1)),
            pl.BlockSpec((tm, B_W), lambda i: (i, 2)),
            pl.BlockSpec((tm, B_W), lambda i: (prev(i), 2)),
            pl.BlockSpec((tm, B_W), lambda i: (i, 3)),
            pl.BlockSpec((tm, B_W), lambda i: (prev(i), 3)),
            pl.BlockSpec((tm, 2 * A_KV), lambda i: (i, (A_Q + 3 * B_W) // (2 * A_KV))),
            pl.BlockSpec((PAIR, 2 * A_KV),
                         lambda i: (jnp.maximum(i * npair - 1, 0), (A_Q + 3 * B_W) // (2 * A_KV))),
            _resident(bias_a.shape, lambda i: (0, 0, 0)),
            _resident(bias_b.shape, lambda i: (0, 0, 0)),
            _resident(sink_rows.shape, lambda i: (0, 0, 0)),
            pl.BlockSpec((None, npair, 1, A_BAND), lambda i: (jnp.minimum(i, 1), 0, 0, 0)),
            pl.BlockSpec((None, npair, 1, B_BAND), lambda i: (jnp.minimum(i, 1), 0, 0, 0)),
        ],
        out_specs=[
            pl.BlockSpec((tm, A_Q), lambda i: (i, 0)),
            pl.BlockSpec((tm, B_W), lambda i: (i, 0)),
        ],
        out_shape=[jax.ShapeDtypeStruct((t, A_Q), BF16), jax.ShapeDtypeStruct((t, B_W), BF16)],
        scratch_shapes=[
            pltpu.VMEM((2 * tm, B_W), BF16),
            pltpu.VMEM((2 * tm, B_W), BF16),
            pltpu.VMEM((PAIR + tm, 2 * A_KV), BF16),
        ],
        compiler_params=_params(),
        name="attn_prompt",
    )(qkv, qkv, qkv, qkv, qkv, qkv, qkv, qkv, bias_a, bias_b, sink_rows, mask_a, mask_b)


def _attn_sample_kernel(qa_ref, qb_ref, kbn_ref, vbn_ref, kvan_ref,
                        cak_ref, cav_ref, cbk_ref, cbv_ref,
                        bias_ac_ref, bias_an_ref, bias_bc_ref, bias_bn_ref, sink_ref,
                        oa_ref, ob_ref):
    tn = qa_ref.shape[0]
    lo = _lane_lo((tn, 2 * HEAD_DIM))
    lo_a = _lane_lo((A_GROUP * tn, 2 * HEAD_DIM))
    kac = cak_ref[...].astype(BF16)
    vac = cav_ref[...].astype(BF16)
    kan = kvan_ref[:, 0:A_KV]
    van = kvan_ref[:, A_KV:2 * A_KV]
    qa = qa_ref[...]
    q_stack = jnp.concatenate(
        [qa[:, p * 2 * HEAD_DIM:(p + 1) * 2 * HEAD_DIM] for p in range(A_GROUP)], axis=0)
    o_kv = []
    for kv in range(A_KV_HEADS):
        qm = jnp.where(lo_a, q_stack, 0) if kv == 0 else jnp.where(lo_a, 0, q_stack)
        s_c = _dot_nt(qm, kac) + bias_ac_ref[kv]
        s_n = _dot_nt(qm, kan) + bias_an_ref[kv]
        o_kv.append(_softmax_pv([s_c, s_n], [vac, van], sink_ref[kv]))
    for p in range(A_GROUP):
        o_pair = jnp.where(lo, o_kv[0][p * tn:(p + 1) * tn], o_kv[1][p * tn:(p + 1) * tn])
        oa_ref[:, p * 2 * HEAD_DIM:(p + 1) * 2 * HEAD_DIM] = o_pair.astype(BF16)
    for hp in range(B_HEADS // 2):
        c0 = hp * 2 * HEAD_DIM
        q2 = qb_ref[:, c0:c0 + 2 * HEAD_DIM]
        kc = cbk_ref[:, c0:c0 + 2 * HEAD_DIM].astype(BF16)
        vc = cbv_ref[:, c0:c0 + 2 * HEAD_DIM].astype(BF16)
        kn = kbn_ref[:, c0:c0 + 2 * HEAD_DIM]
        vn = vbn_ref[:, c0:c0 + 2 * HEAD_DIM]
        o_h = []
        for half in range(2):
            qm = jnp.where(lo, q2, 0) if half == 0 else jnp.where(lo, 0, q2)
            s_c = _dot_nt(qm, kc) + bias_bc_ref[2 * hp + half]
            s_n = _dot_nt(qm, kn) + bias_bn_ref[2 * hp + half]
            o_h.append(_softmax_pv([s_c, s_n], [vc, vn]))
        ob_ref[:, c0:c0 + 2 * HEAD_DIM] = jnp.where(lo, o_h[0], o_h[1]).astype(BF16)


def _attn_sample(qkv, cak, cav, cbk, cbv, layer, tn, bias_ac, bias_an, bias_bc, bias_bn, sink_rows):
    t = qkv.shape[0]
    nb = t // tn
    la = cak.shape[2]
    lb = cbk.shape[2]
    kva_blk = (A_Q + 3 * B_W) // (2 * A_KV)
    const3 = lambda b: (0, 0, 0)
    return pl.pallas_call(
        _attn_sample_kernel,
        grid=(nb,),
        in_specs=[
            pl.BlockSpec((tn, A_Q), lambda b: (b, 0)),
            pl.BlockSpec((tn, B_W), lambda b: (b, 1)),
            pl.BlockSpec((tn, B_W), lambda b: (b, 2)),
            pl.BlockSpec((tn, B_W), lambda b: (b, 3)),
            pl.BlockSpec((tn, 2 * A_KV), lambda b: (b, kva_blk)),
            pl.BlockSpec((None, None, la, A_KV), lambda b: (layer, b, 0, 0)),
            pl.BlockSpec((None, None, la, A_KV), lambda b: (layer, b, 0, 0)),
            pl.BlockSpec((None, None, lb, B_W), lambda b: (layer, b, 0, 0)),
            pl.BlockSpec((None, None, lb, B_W), lambda b: (layer, b, 0, 0)),
            _resident(bias_ac.shape, const3),
            _resident(bias_an.shape, const3),
            _resident(bias_bc.shape, const3),
            _resident(bias_bn.shape, const3),
            _resident(sink_rows.shape, const3),
        ],
        out_specs=[
            pl.BlockSpec((tn, A_Q), lambda b: (b, 0)),
            pl.BlockSpec((tn, B_W), lambda b: (b, 0)),
        ],
        out_shape=[jax.ShapeDtypeStruct((t, A_Q), BF16), jax.ShapeDtypeStruct((t, B_W), BF16)],
        compiler_params=_params(),
        name="attn_sample",
    )(qkv, qkv, qkv, qkv, qkv, cak, cav, cbk, cbv, bias_ac, bias_an, bias_bc, bias_bn, sink_rows)


def _merge_kernel(x_ref, oa_ref, ob_ref, gate_ref, woa_ref, wob_ref, wout_ref, o_ref):
    d = x_ref.shape[1]
    a = _dot(oa_ref[...], woa_ref[...])
    b = _dot(ob_ref[...], wob_ref[...])
    m = jax.nn.sigmoid(gate_ref[:, 0:d]) * a + jax.nn.sigmoid(gate_ref[:, d:2 * d]) * b
    o_ref[...] = x_ref[...] + _dot(m.astype(BF16), wout_ref[...])


def _merge(x, oa, ob, gate, woa, wob, wout, tm):
    t, d = x.shape
    row = lambda i: (i, 0)
    const = lambda i: (0, 0)
    return pl.pallas_call(
        _merge_kernel,
        grid=(t // tm,),
        in_specs=[
            pl.BlockSpec((tm, d), row),
            pl.BlockSpec((tm, A_Q), row),
            pl.BlockSpec((tm, B_W), row),
            pl.BlockSpec((tm, 2 * d), row),
            _resident(woa.shape, const),
            _resident(wob.shape, const),
            _resident(wout.shape, const),
        ],
        out_specs=pl.BlockSpec((tm, d), row),
        out_shape=jax.ShapeDtypeStruct((t, d), F32),
        compiler_params=_params(),
        name="merge",
    )(x, oa, ob, gate, woa, wob, wout)


def _cross_kernel(x_ref, g_ref, wq_ref, mk_ref, mv_ref, wo_ref, o_ref):
    x = x_ref[...]
    h = _rmsnorm(x, g_ref[...]).astype(BF16)
    q = _dot(h, wq_ref[...]).astype(BF16)
    mk = mk_ref[...].astype(BF16)
    mv = mv_ref[...].astype(BF16)
    scale = X_HEAD_DIM ** -0.5
    outs = []
    for hd in range(X_HEADS):
        c0 = hd * X_HEAD_DIM
        s = _dot_nt(q[:, c0:c0 + X_HEAD_DIM], mk[:, c0:c0 + X_HEAD_DIM]) * scale
        outs.append(_softmax_pv([s], [mv[:, c0:c0 + X_HEAD_DIM]]))
    o = jnp.concatenate(outs, axis=-1).astype(BF16)
    o_ref[...] = x + _dot(o, wo_ref[...])


def _cross(x3, g, wq, mk, mv, wo, tm, kv_index):
    nb, rows, d = x3.shape
    n_mem = mk.shape[-2]
    kv_block = (None,) * (mk.ndim - 2) + (n_mem, X_W)
    const = lambda b, i: (0, 0)
    return pl.pallas_call(
        _cross_kernel,
        grid=(nb, rows // tm),
        in_specs=[
            pl.BlockSpec((None, tm, d), lambda b, i: (b, i, 0)),
            _resident((1, d), const),
            _resident(wq.shape, const),
            pl.BlockSpec(kv_block, lambda b, i: kv_index(b)),
            pl.BlockSpec(kv_block, lambda b, i: kv_index(b)),
            _resident(wo.shape, const),
        ],
        out_specs=pl.BlockSpec((None, tm, d), lambda b, i: (b, i, 0)),
        out_shape=jax.ShapeDtypeStruct((nb, rows, d), F32),
        compiler_params=_params(2),
        name="cross",
    )(x3, g.reshape(1, d), wq, mk, mv, wo)


def _ffn_kernel(*refs, nb, tm, d_ff, final):
    if final:
        (x_ref, g_ref, state_ref, wup_ref, wconv_ref, bconv_ref, wdown_ref, gfin_ref,
         o_ref, ctail_ref, carry, ebuf, act) = refs
    else:
        (x_ref, g_ref, state_ref, wup_ref, wconv_ref, bconv_ref, wdown_ref,
         o_ref, ctail_ref, carry, ebuf, act) = refs
    i = pl.program_id(0)
    d = x_ref.shape[-1]
    hist = CONV_W - 1
    top = 8

    @pl.when(i == 0)
    def _():
        carry[:, top - hist:top, :] = state_ref[...]

    x = x_ref[...].reshape(nb * tm, d)
    h = _rmsnorm(x, g_ref[...]).astype(BF16)
    for j in range(d_ff // FF_CHUNK):
        c_half = []
        for half in range(2):
            c0 = half * d_ff + j * FF_CHUNK
            u = _dot(h, wup_ref[:, c0:c0 + FF_CHUNK])
            ebuf[half, :, top - hist:top, :] = carry[:, top - hist:top, c0:c0 + FF_CHUNK]
            ebuf[half, :, top:top + tm, :] = u.reshape(nb, tm, FF_CHUNK)
            carry[:, top - hist:top, c0:c0 + FF_CHUNK] = ebuf[half, :, top + tm - hist:top + tm, :]
            c = bconv_ref[:, c0:c0 + FF_CHUNK]
            for k in range(CONV_W):
                c = c + ebuf[half, :, top - hist + k:top - hist + k + tm, :] * wconv_ref[k:k + 1, c0:c0 + FF_CHUNK]
            c_half.append(c)
        a = jax.nn.silu(c_half[0]) * c_half[1]
        act[:, j * FF_CHUNK:(j + 1) * FF_CHUNK] = a.reshape(nb * tm, FF_CHUNK).astype(BF16)
    out = x + _dot(act[...], wdown_ref[...])
    if final:
        out = _rmsnorm(out, gfin_ref[...])
    o_ref[...] = out.reshape(nb, tm, d)

    @pl.when(i == pl.num_programs(0) - 1)
    def _():
        ctail_ref[...] = carry[:, top - hist:top, :]


def _ffn(x3, g, state, wup, wconv, bconv, wdown, gfin, tm):
    nb, rows, d = x3.shape
    d_ff = wdown.shape[0]
    final = gfin is not None
    const = lambda i: (0, 0)
    in_specs = [
        pl.BlockSpec((nb, tm, d), lambda i: (0, i, 0)),
        _resident((1, d), const),
        _resident(state.shape, lambda i: (0, 0, 0)),
        _resident(wup.shape, const),
        _resident(wconv.shape, const),
        _resident((1, 2 * d_ff), const),
        _resident(wdown.shape, const),
    ]
    args = [x3, g.reshape(1, d), state, wup, wconv, bconv.reshape(1, 2 * d_ff), wdown]
    if final:
        in_specs.append(_resident((1, d), const))
        args.append(gfin.reshape(1, d))
    return pl.pallas_call(
        functools.partial(_ffn_kernel, nb=nb, tm=tm, d_ff=d_ff, final=final),
        grid=(rows // tm,),
        in_specs=in_specs,
        out_specs=[
            pl.BlockSpec((nb, tm, d), lambda i: (0, i, 0)),
            pl.BlockSpec((nb, CONV_W - 1, 2 * d_ff), lambda i: (0, 0, 0)),
        ],
        out_shape=[
            jax.ShapeDtypeStruct((nb, rows, d), F32),
            jax.ShapeDtypeStruct((nb, CONV_W - 1, 2 * d_ff), F32),
        ],
        scratch_shapes=[
            pltpu.VMEM((nb, 8, 2 * d_ff), F32),
            pltpu.VMEM((2, nb, 8 + tm, FF_CHUNK), F32),
            pltpu.VMEM((nb * tm, d_ff), BF16),
        ],
        compiler_params=_params(),
        name="conv_ffn",
    )(*args)


def _alibi_slopes():
    return np.exp2(-8.0 * np.arange(1, A_HEADS + 1) / A_HEADS).astype(np.float32)


def _pair_major(w, axis):
    shp = w.shape
    w = w.reshape(shp[:axis] + (A_KV_HEADS, A_GROUP, HEAD_DIM) + shp[axis + 1:])
    w = jnp.swapaxes(w, axis, axis + 1)
    return w.reshape(shp)


def _prep_w_in(w):
    d = w.shape[1]
    c = np.cumsum([0, A_Q, A_KV, A_KV, B_W, B_W, B_W, d, d])
    qa, ka, va, qb, kb, vb, ga, gb = [w[..., c[i]:c[i + 1]] for i in range(8)]
    qscale = HEAD_DIM ** -0.5
    qa = _pair_major(qa, 2) * qscale
    return jnp.concatenate([ga, gb, qa, qb * qscale, kb, vb, ka, va], axis=-1).astype(BF16)


def _prompt_bias_a():
    r = np.arange(PAIR)[:, None]
    c = np.arange(A_BAND)[None, :]
    dist = np.abs(A_BACK * CHUNK + r - c).astype(np.float32)
    qc, kc = r // CHUNK, c // CHUNK
    valid = np.where((kc >= qc) & (kc <= qc + A_BACK), 0.0, NEG).astype(np.float32)
    slopes = _alibi_slopes().reshape(A_KV_HEADS, A_GROUP, 1, 1)
    bias = -slopes * dist + valid
    return jnp.asarray(bias.reshape(A_KV_HEADS, A_GROUP * PAIR, A_BAND))


def _prompt_bias_b(table):
    r = np.arange(PAIR)[:, None]
    c = np.arange(B_BAND)[None, :]
    idx = np.clip(c - B_BACK * CHUNK - r, -REL_CLIP, REL_CLIP) + REL_CLIP
    qc, kc = r // CHUNK, c // CHUNK
    valid = np.where((kc >= qc) & (kc <= qc + B_BACK), 0.0, NEG).astype(np.float32)
    return table[:, idx] + valid


def _prompt_start_masks(tm):
    npair = tm // PAIR
    start = np.arange(npair)[:, None] * PAIR
    col_a = start + np.arange(A_BAND)[None, :]
    col_b = start + np.arange(B_BAND)[None, :]
    first_a = np.where(col_a < PAIR, NEG, 0.0)
    first_b = np.where(col_b < tm, NEG, 0.0)
    mask_a = np.stack([first_a, np.zeros_like(first_a)]).astype(np.float32)
    mask_b = np.stack([first_b, np.zeros_like(first_b)]).astype(np.float32)
    return (jnp.asarray(mask_a.reshape(2, npair, 1, A_BAND)),
            jnp.asarray(mask_b.reshape(2, npair, 1, B_BAND)))


def _sample_bias_a(past_len, la, tn):
    pos_q = past_len + np.arange(tn)
    slopes = _alibi_slopes().reshape(A_KV_HEADS, A_GROUP, 1, 1)

    def bias(pos_k):
        dist = np.abs(pos_q[:, None] - pos_k[None, :]).astype(np.float32)
        cq, ck = pos_q[:, None] // CHUNK, pos_k[None, :] // CHUNK
        valid = np.where((ck <= cq) & (ck >= cq - A_BACK), 0.0, NEG).astype(np.float32)
        b = -slopes * dist + valid
        return jnp.asarray(b.reshape(A_KV_HEADS, A_GROUP * tn, pos_k.shape[0]))

    return bias(past_len - la + np.arange(la)), bias(pos_q)


def _sample_bias_b(table, past_len, lb, tn):
    pos_q = past_len + np.arange(tn)

    def bias(pos_k):
        idx = np.clip(pos_k[None, :] - pos_q[:, None], -REL_CLIP, REL_CLIP) + REL_CLIP
        cq, ck = pos_q[:, None] // CHUNK, pos_k[None, :] // CHUNK
        valid = np.where((ck <= cq) & (ck >= cq - B_BACK), 0.0, NEG).astype(np.float32)
        return table[:, idx] + valid

    return bias(past_len - lb + np.arange(lb)), bias(pos_q)


def _sink_rows(sink, rows):
    s = jnp.repeat(sink.reshape(A_KV_HEADS, A_GROUP, 1), rows, axis=2)
    return s.reshape(A_KV_HEADS, A_GROUP * rows, 1)


def kernel(x_prompt, x_sample, mem_prompt, cache_a_k, cache_a_v, cache_b_k, cache_b_v, cache_mem_k, cache_mem_v, state_conv, g_mix, w_mix_in, a_sink, b_rel_bias, w_o_a, w_o_b, w_mix_out, g_xattn, g_mem, w_xq, w_xk, w_xv, w_xo, g_ffn, w_up, w_conv, b_conv, w_down, g_final):
    batch, seq, d = x_prompt.shape
    dec_batch, dec_seq, _ = x_sample.shape
    depth = w_mix_in.shape[0]
    n_mem = mem_prompt.shape[1]
    d_ff = w_down.shape[1]
    la, lb = cache_a_k.shape[2], cache_b_k.shape[2]
    past_len = PAST_LEN
    assert batch == 1 and seq % ROW_TILE == 0 and d_ff % FF_CHUNK == 0
    na, nbk = min(A_BACK * CHUNK, seq), min(B_BACK * CHUNK, seq)
    assert nbk == ROW_TILE and na <= ROW_TILE
    ts = dec_batch * dec_seq

    w_in = _prep_w_in(w_mix_in)
    woa = _pair_major(w_o_a, 1).astype(BF16)
    wob = w_o_b.astype(BF16)
    wout = w_mix_out.astype(BF16)
    wxq, wxk, wxv, wxo = (w.astype(BF16) for w in (w_xq, w_xk, w_xv, w_xo))
    wup = w_up.astype(BF16)
    wdown = w_down.astype(BF16)

    bias_a_p = _prompt_bias_a()
    mask_a_p, mask_b_p = _prompt_start_masks(ROW_TILE)
    bias_a_sc, bias_a_sn = _sample_bias_a(past_len, la, dec_seq)
    cak = cache_a_k.reshape(depth, dec_batch, la, A_KV)
    cav = cache_a_v.reshape(depth, dec_batch, la, A_KV)
    cbk = cache_b_k.reshape(depth, dec_batch, lb, B_W)
    cbv = cache_b_v.reshape(depth, dec_batch, lb, B_W)
    cmk = cache_mem_k.reshape(depth, dec_batch, n_mem, X_W)
    cmv = cache_mem_v.reshape(depth, dec_batch, n_mem, X_W)
    zero_state = jnp.zeros((batch, CONV_W - 1, 2 * d_ff), F32)

    pm_k, pm_v = _memkv(mem_prompt[0], g_mem, wxk, wxv)

    xp = x_prompt.reshape(seq, d)
    xs = x_sample.reshape(ts, d)
    outs = {k: [] for k in ("pa_k", "pa_v", "pb_k", "pb_v", "pconv", "sa_k", "sa_v", "sb_k", "sb_v", "sconv")}
    for l in range(depth):
        gfin = g_final if l == depth - 1 else None
        rel = b_rel_bias[l].astype(F32)
        qkv, gate, tail = _inproj(xp, g_mix[l], w_in[l], ROW_TILE)
        oa, ob = _attn_prompt(qkv, bias_a_p, _prompt_bias_b(rel), _sink_rows(a_sink[l], PAIR),
                              mask_a_p, mask_b_p)
        xp = _merge(xp, oa, ob, gate, woa[l], wob[l], wout[l], ROW_TILE)
        xp = _cross(xp.reshape(batch, seq, d), g_xattn[l], wxq[l], pm_k, pm_v, wxo[l], ROW_TILE,
                    lambda b, l=l: (l, 0, 0))
        xp, cp = _ffn(xp, g_ffn[l], zero_state, wup[l], w_conv[l], b_conv[l], wdown[l], gfin, ROW_TILE)
        xp = xp.reshape(seq, d)
        outs["pb_k"].append(tail[ROW_TILE - nbk:, 0:B_W].reshape(batch, nbk, B_HEADS, HEAD_DIM))
        outs["pb_v"].append(tail[ROW_TILE - nbk:, B_W:2 * B_W].reshape(batch, nbk, B_HEADS, HEAD_DIM))
        outs["pa_k"].append(tail[ROW_TILE - na:, 2 * B_W:2 * B_W + A_KV].reshape(batch, na, A_KV_HEADS, HEAD_DIM))
        outs["pa_v"].append(tail[ROW_TILE - na:, 2 * B_W + A_KV:].reshape(batch, na, A_KV_HEADS, HEAD_DIM))
        outs["pconv"].append(cp)
        qkv, gate, tail = _inproj(xs, g_mix[l], w_in[l], ts)
        bias_b_sc, bias_b_sn = _sample_bias_b(rel, past_len, lb, dec_seq)
        oa, ob = _attn_sample(qkv, cak, cav, cbk, cbv, l, dec_seq, bias_a_sc, bias_a_sn,
                              bias_b_sc, bias_b_sn, _sink_rows(a_sink[l], dec_seq))
        xs = _merge(xs, oa, ob, gate, woa[l], wob[l], wout[l], ts)
        xs = _cross(xs.reshape(dec_batch, dec_seq, d), g_xattn[l], wxq[l], cmk, cmv, wxo[l], dec_seq,
                    lambda b, l=l: (l, b, 0, 0))
        xs, cs = _ffn(xs, g_ffn[l], state_conv[l], wup[l], w_conv[l], b_conv[l], wdown[l], gfin, dec_seq)
        xs = xs.reshape(ts, d)
        outs["sb_k"].append(tail[:, 0:B_W].reshape(dec_batch, dec_seq, B_HEADS, HEAD_DIM))
        outs["sb_v"].append(tail[:, B_W:2 * B_W].reshape(dec_batch, dec_seq, B_HEADS, HEAD_DIM))
        outs["sa_k"].append(tail[:, 2 * B_W:2 * B_W + A_KV].reshape(dec_batch, dec_seq, A_KV_HEADS, HEAD_DIM))
        outs["sa_v"].append(tail[:, 2 * B_W + A_KV:].reshape(dec_batch, dec_seq, A_KV_HEADS, HEAD_DIM))
        outs["sconv"].append(cs)

    st = {k: jnp.stack(v) for k, v in outs.items()}
    pm_k = pm_k.reshape(depth, batch, n_mem, X_HEADS, X_HEAD_DIM)
    pm_v = pm_v.reshape(depth, batch, n_mem, X_HEADS, X_HEAD_DIM)
    return (xp.reshape(batch, seq, d), xs.reshape(dec_batch, dec_seq, d),
            st["pa_k"], st["pa_v"], st["pb_k"], st["pb_v"], pm_k, pm_v, st["pconv"],
            st["sa_k"], st["sa_v"], st["sb_k"], st["sb_v"], st["sconv"])
```

```python
import functools

import numpy as np
import jax
import jax.numpy as jnp
from jax import lax
from jax.experimental import pallas as pl
from jax.experimental.pallas import tpu as pltpu

CHUNK = 64
HEAD_DIM = 64
A_HEADS = 8
A_KV_HEADS = 2
A_GROUP = A_HEADS // A_KV_HEADS
A_BACK = 2
B_HEADS = 8
B_BACK = 8
REL_CLIP = 256
X_HEADS = 4
X_HEAD_DIM = 128
EPS = 1e-6
NEG = -1e30
LOG2E = float(np.log2(np.e))

A_Q = A_HEADS * HEAD_DIM
A_KV = A_KV_HEADS * HEAD_DIM
B_W = B_HEADS * HEAD_DIM
X_W = X_HEADS * X_HEAD_DIM
CONV_W = 3
PAST_LEN = 1024

PAIR = 2 * CHUNK
ROW_TILE = 512
A_BAND = (A_BACK + 2) * CHUNK
B_BAND = (B_BACK + 2) * CHUNK
FF_CHUNK = 256

V7X_VMEM_LIMIT = 56 * 1024 * 1024

F32 = jnp.float32
BF16 = jnp.bfloat16


def _params(n_axes=1):
    return pltpu.CompilerParams(
        dimension_semantics=("arbitrary",) * n_axes,
        vmem_limit_bytes=V7X_VMEM_LIMIT)


def _resident(shape, index_map):
    return pl.BlockSpec(shape, index_map, pipeline_mode=pl.Buffered(1))


def _rmsnorm(x, g):
    ms = jnp.mean(x * x, axis=-1, keepdims=True)
    return (x * lax.rsqrt(ms + EPS)) * g


def _dot(a, b):
    return jnp.dot(a, b, preferred_element_type=F32)


def _dot_nt(a, b):
    return lax.dot_general(a, b, (((1,), (1,)), ((), ())), preferred_element_type=F32)


def _softmax_pv(s_list, v_list, sink=None):
    m = s_list[0].max(axis=-1, keepdims=True)
    for s in s_list[1:]:
        m = jnp.maximum(m, s.max(axis=-1, keepdims=True))
    if sink is not None:
        m = jnp.maximum(m, sink)
    l = None
    o = None
    for s, v in zip(s_list, v_list):
        p = jnp.exp2(s - m)
        ls = p.sum(axis=-1, keepdims=True)
        os_ = _dot(p.astype(BF16), v)
        l = ls if l is None else l + ls
        o = os_ if o is None else o + os_
    if sink is not None:
        l = l + jnp.exp2(sink - m)
    return o * (1.0 / l)


def _lane_lo(shape):
    return lax.broadcasted_iota(jnp.int32, shape, len(shape) - 1) < HEAD_DIM


def _memkv_kernel(mem_ref, g_ref, wk_ref, wv_ref, k_ref, v_ref):
    hm = _rmsnorm(mem_ref[...], g_ref[...]).astype(BF16)
    k_ref[...] = _dot(hm, wk_ref[...])
    v_ref[...] = _dot(hm, wv_ref[...])


def _memkv(mem, g_mem, w_xk, w_xv):
    depth, d, xw = w_xk.shape
    n_mem = mem.shape[0]
    return pl.pallas_call(
        _memkv_kernel,
        grid=(depth,),
        in_specs=[
            pl.BlockSpec((n_mem, d), lambda l: (0, 0)),
            pl.BlockSpec((None, 1, d), lambda l: (l, 0, 0)),
            pl.BlockSpec((None, d, xw), lambda l: (l, 0, 0)),
            pl.BlockSpec((None, d, xw), lambda l: (l, 0, 0)),
        ],
        out_specs=[
            pl.BlockSpec((None, n_mem, xw), lambda l: (l, 0, 0)),
            pl.BlockSpec((None, n_mem, xw), lambda l: (l, 0, 0)),
        ],
        out_shape=[jax.ShapeDtypeStruct((depth, n_mem, xw), F32)] * 2,
        compiler_params=_params(),
        name="mem_kv",
    )(mem, g_mem.reshape(depth, 1, d), w_xk, w_xv)


N_GATE = 2048
N_QKV = A_Q + 3 * B_W + 2 * A_KV
N_TAIL = 2 * B_W + 2 * A_KV


def _inproj_kernel(x_ref, g_ref, w_ref, qkv_ref, gate_ref, tail_ref):
    h = _rmsnorm(x_ref[...], g_ref[...]).astype(BF16)
    gate_ref[...] = _dot(h, w_ref[:, :N_GATE])
    z = _dot(h, w_ref[:, N_GATE:])
    qkv_ref[...] = z.astype(BF16)

    @pl.when(pl.program_id(0) == pl.num_programs(0) - 1)
    def _():
        tail_ref[...] = z[:, N_QKV - N_TAIL:]


def _inproj(x, g, w, tm):
    t, d = x.shape
    n = w.shape[1]
    return pl.pallas_call(
        _inproj_kernel,
        grid=(t // tm,),
        in_specs=[
            pl.BlockSpec((tm, d), lambda i: (i, 0)),
            _resident((1, d), lambda i: (0, 0)),
            _resident((d, n), lambda i: (0, 0)),
        ],
        out_specs=[
            pl.BlockSpec((tm, N_QKV), lambda i: (i, 0)),
            pl.BlockSpec((tm, N_GATE), lambda i: (i, 0)),
            pl.BlockSpec((tm, N_TAIL), lambda i: (0, 0)),
        ],
        out_shape=[
            jax.ShapeDtypeStruct((t, N_QKV), BF16),
            jax.ShapeDtypeStruct((t, N_GATE), F32),
            jax.ShapeDtypeStruct((tm, N_TAIL), F32),
        ],
        compiler_params=_params(),
        name="inproj",
    )(x, g.reshape(1, d), w)


def _attn_prompt_kernel(qa_ref, qb_ref, kbc_ref, kbp_ref, vbc_ref, vbp_ref,
                        kvac_ref, kvap_ref, bias_a_ref, bias_b_ref, sink_ref,
                        oa_ref, ob_ref, kcat, vcat, kvacat):
    tm = qa_ref.shape[0]
    npair = tm // PAIR
    first_tile = pl.program_id(0) == 0
    kcat[0:tm, :] = kbp_ref[...]
    kcat[tm:2 * tm, :] = kbc_ref[...]
    vcat[0:tm, :] = vbp_ref[...]
    vcat[tm:2 * tm, :] = vbc_ref[...]
    kvacat[0:PAIR, :] = kvap_ref[...]
    kvacat[PAIR:PAIR + tm, :] = kvac_ref[...]
    lo = _lane_lo((PAIR, 2 * HEAD_DIM))
    lo_a = _lane_lo((A_GROUP * PAIR, 2 * HEAD_DIM))

    def pair_body(d, carry):
        r0 = pl.multiple_of(d * PAIR, PAIR)
        var = jnp.where(first_tile, d, npair)
        ka2 = kvacat[pl.ds(r0, A_BAND), 0:A_KV]
        va2 = kvacat[pl.ds(r0, A_BAND), A_KV:2 * A_KV]
        qa = qa_ref[pl.ds(r0, PAIR), :]
        q_stack = jnp.concatenate(
            [qa[:, p * 2 * HEAD_DIM:(p + 1) * 2 * HEAD_DIM] for p in range(A_GROUP)], axis=0)
        o_kv = []
        for kv in range(A_KV_HEADS):
            qm = jnp.where(lo_a, q_stack, 0) if kv == 0 else jnp.where(lo_a, 0, q_stack)
            s = _dot_nt(qm, ka2) + bias_a_ref[var, kv]
            o_kv.append(_softmax_pv([s], [va2], sink_ref[kv]))
        for p in range(A_GROUP):
            o_pair = jnp.where(lo, o_kv[0][p * PAIR:(p + 1) * PAIR], o_kv[1][p * PAIR:(p + 1) * PAIR])
            oa_ref[pl.ds(r0, PAIR), p * 2 * HEAD_DIM:(p + 1) * 2 * HEAD_DIM] = o_pair.astype(BF16)
        for hp in range(B_HEADS // 2):
            c0 = hp * 2 * HEAD_DIM
            q2 = qb_ref[pl.ds(r0, PAIR), c0:c0 + 2 * HEAD_DIM]
            k2 = kcat[pl.ds(r0, B_BAND), c0:c0 + 2 * HEAD_DIM]
            v2 = vcat[pl.ds(r0, B_BAND), c0:c0 + 2 * HEAD_DIM]
            o_h = []
            for half in range(2):
                qm = jnp.where(lo, q2, 0) if half == 0 else jnp.where(lo, 0, q2)
                s = _dot_nt(qm, k2) + bias_b_ref[var, 2 * hp + half]
                o_h.append(_softmax_pv([s], [v2]))
            ob_ref[pl.ds(r0, PAIR), c0:c0 + 2 * HEAD_DIM] = jnp.where(lo, o_h[0], o_h[1]).astype(BF16)
        return carry

    lax.fori_loop(0, tm // PAIR, pair_body, 0)


def _attn_prompt(qkv, bias_a, bias_b, layer, sink_rows):
    t = qkv.shape[0]
    tm = ROW_TILE
    npair = tm // PAIR
    prev = lambda i: jnp.maximum(i - 1, 0)
    return pl.pallas_call(
        _attn_prompt_kernel,
        grid=(t // tm,),
        in_specs=[
            pl.BlockSpec((tm, A_Q), lambda i: (i, 0)),
            pl.BlockSpec((tm, B_W), lambda i: (i, 1)),
            pl.BlockSpec((tm, B_W), lambda i: (i, 2)),
            pl.BlockSpec((tm, B_W), lambda i: (prev(i), 2)),
            pl.BlockSpec((tm, B_W), lambda i: (i, 3)),
            pl.BlockSpec((tm, B_W), lambda i: (prev(i), 3)),
            pl.BlockSpec((tm, 2 * A_KV), lambda i: (i, (A_Q + 3 * B_W) // (2 * A_KV))),
            pl.BlockSpec((PAIR, 2 * A_KV),
                         lambda i: (jnp.maximum(i * npair - 1, 0), (A_Q + 3 * B_W) // (2 * A_KV))),
            _resident(bias_a.shape, lambda i: (0, 0, 0, 0)),
            _resident((None,) + bias_b.shape[1:], lambda i: (layer, 0, 0, 0, 0)),
            _resident(sink_rows.shape, lambda i: (0, 0, 0)),
        ],
        out_specs=[
            pl.BlockSpec((tm, A_Q), lambda i: (i, 0)),
            pl.BlockSpec((tm, B_W), lambda i: (i, 0)),
        ],
        out_shape=[jax.ShapeDtypeStruct((t, A_Q), BF16), jax.ShapeDtypeStruct((t, B_W), BF16)],
        scratch_shapes=[
            pltpu.VMEM((2 * tm, B_W), BF16),
            pltpu.VMEM((2 * tm, B_W), BF16),
            pltpu.VMEM((PAIR + tm, 2 * A_KV), BF16),
        ],
        compiler_params=_params(),
        name="attn_prompt",
    )(qkv, qkv, qkv, qkv, qkv, qkv, qkv, qkv, bias_a, bias_b, sink_rows)


def _attn_sample_kernel(qa_ref, qb_ref, kbn_ref, vbn_ref, kvan_ref,
                        cak_ref, cav_ref, cbk_ref, cbv_ref,
                        bias_ac_ref, bias_an_ref, bias_bc_ref, bias_bn_ref, sink_ref,
                        oa_ref, ob_ref):
    tn = qa_ref.shape[0]
    lo = _lane_lo((tn, 2 * HEAD_DIM))
    lo_a = _lane_lo((A_GROUP * tn, 2 * HEAD_DIM))
    kac = cak_ref[...].astype(BF16)
    vac = cav_ref[...].astype(BF16)
    kan = kvan_ref[:, 0:A_KV]
    van = kvan_ref[:, A_KV:2 * A_KV]
    qa = qa_ref[...]
    q_stack = jnp.concatenate(
        [qa[:, p * 2 * HEAD_DIM:(p + 1) * 2 * HEAD_DIM] for p in range(A_GROUP)], axis=0)
    o_kv = []
    for kv in range(A_KV_HEADS):
        qm = jnp.where(lo_a, q_stack, 0) if kv == 0 else jnp.where(lo_a, 0, q_stack)
        s_c = _dot_nt(qm, kac) + bias_ac_ref[kv]
        s_n = _dot_nt(qm, kan) + bias_an_ref[kv]
        o_kv.append(_softmax_pv([s_c, s_n], [vac, van], sink_ref[kv]))
    for p in range(A_GROUP):
        o_pair = jnp.where(lo, o_kv[0][p * tn:(p + 1) * tn], o_kv[1][p * tn:(p + 1) * tn])
        oa_ref[:, p * 2 * HEAD_DIM:(p + 1) * 2 * HEAD_DIM] = o_pair.astype(BF16)
    for hp in range(B_HEADS // 2):
        c0 = hp * 2 * HEAD_DIM
        q2 = qb_ref[:, c0:c0 + 2 * HEAD_DIM]
        kc = cbk_ref[:, c0:c0 + 2 * HEAD_DIM].astype(BF16)
        vc = cbv_ref[:, c0:c0 + 2 * HEAD_DIM].astype(BF16)
        kn = kbn_ref[:, c0:c0 + 2 * HEAD_DIM]
        vn = vbn_ref[:, c0:c0 + 2 * HEAD_DIM]
        o_h = []
        for half in range(2):
            qm = jnp.where(lo, q2, 0) if half == 0 else jnp.where(lo, 0, q2)
            s_c = _dot_nt(qm, kc) + bias_bc_ref[2 * hp + half]
            s_n = _dot_nt(qm, kn) + bias_bn_ref[2 * hp + half]
            o_h.append(_softmax_pv([s_c, s_n], [vc, vn]))
        ob_ref[:, c0:c0 + 2 * HEAD_DIM] = jnp.where(lo, o_h[0], o_h[1]).astype(BF16)


def _attn_sample(qkv, cak, cav, cbk, cbv, layer, tn, bias_ac, bias_an, bias_bc, bias_bn, sink_rows):
    t = qkv.shape[0]
    nb = t // tn
    la = cak.shape[2]
    lb = cbk.shape[2]
    kva_blk = (A_Q + 3 * B_W) // (2 * A_KV)
    const3 = lambda b: (0, 0, 0)
    return pl.pallas_call(
        _attn_sample_kernel,
        grid=(nb,),
        in_specs=[
            pl.BlockSpec((tn, A_Q), lambda b: (b, 0)),
            pl.BlockSpec((tn, B_W), lambda b: (b, 1)),
            pl.BlockSpec((tn, B_W), lambda b: (b, 2)),
            pl.BlockSpec((tn, B_W), lambda b: (b, 3)),
            pl.BlockSpec((tn, 2 * A_KV), lambda b: (b, kva_blk)),
            pl.BlockSpec((None, None, la, A_KV), lambda b: (layer, b, 0, 0)),
            pl.BlockSpec((None, None, la, A_KV), lambda b: (layer, b, 0, 0)),
            pl.BlockSpec((None, None, lb, B_W), lambda b: (layer, b, 0, 0)),
            pl.BlockSpec((None, None, lb, B_W), lambda b: (layer, b, 0, 0)),
            _resident(bias_ac.shape, const3),
            _resident(bias_an.shape, const3),
            _resident(bias_bc.shape, const3),
            _resident(bias_bn.shape, const3),
            _resident(sink_rows.shape, const3),
        ],
        out_specs=[
            pl.BlockSpec((tn, A_Q), lambda b: (b, 0)),
            pl.BlockSpec((tn, B_W), lambda b: (b, 0)),
        ],
        out_shape=[jax.ShapeDtypeStruct((t, A_Q), BF16), jax.ShapeDtypeStruct((t, B_W), BF16)],
        compiler_params=_params(),
        name="attn_sample",
    )(qkv, qkv, qkv, qkv, qkv, cak, cav, cbk, cbv, bias_ac, bias_an, bias_bc, bias_bn, sink_rows)


def _merge_kernel(x_ref, oa_ref, ob_ref, gate_ref, woa_ref, wob_ref, wout_ref, o_ref):
    d = x_ref.shape[1]
    a = _dot(oa_ref[...], woa_ref[...])
    b = _dot(ob_ref[...], wob_ref[...])
    m = jax.nn.sigmoid(gate_ref[:, 0:d]) * a + jax.nn.sigmoid(gate_ref[:, d:2 * d]) * b
    o_ref[...] = x_ref[...] + _dot(m.astype(BF16), wout_ref[...])


def _merge(x, oa, ob, gate, woa, wob, wout, tm):
    t, d = x.shape
    row = lambda i: (i, 0)
    const = lambda i: (0, 0)
    return pl.pallas_call(
        _merge_kernel,
        grid=(t // tm,),
        in_specs=[
            pl.BlockSpec((tm, d), row),
            pl.BlockSpec((tm, A_Q), row),
            pl.BlockSpec((tm, B_W), row),
            pl.BlockSpec((tm, 2 * d), row),
            _resident(woa.shape, const),
            _resident(wob.shape, const),
            _resident(wout.shape, const),
        ],
        out_specs=pl.BlockSpec((tm, d), row),
        out_shape=jax.ShapeDtypeStruct((t, d), F32),
        compiler_params=_params(),
        name="merge",
    )(x, oa, ob, gate, woa, wob, wout)


def _cross_kernel(x_ref, g_ref, wq_ref, mk_ref, mv_ref, wo_ref, o_ref):
    x = x_ref[...]
    h = _rmsnorm(x, g_ref[...]).astype(BF16)
    q = _dot(h, wq_ref[...]).astype(BF16)
    mk = mk_ref[...].astype(BF16)
    mv = mv_ref[...].astype(BF16)
    scale = X_HEAD_DIM ** -0.5 * LOG2E
    outs = []
    for hd in range(X_HEADS):
        c0 = hd * X_HEAD_DIM
        s = _dot_nt(q[:, c0:c0 + X_HEAD_DIM], mk[:, c0:c0 + X_HEAD_DIM]) * scale
        outs.append(_softmax_pv([s], [mv[:, c0:c0 + X_HEAD_DIM]]))
    o = jnp.concatenate(outs, axis=-1).astype(BF16)
    o_ref[...] = x + _dot(o, wo_ref[...])


def _cross(x3, g, wq, mk, mv, wo, tm, kv_index):
    nb, rows, d = x3.shape
    n_mem = mk.shape[-2]
    kv_block = (None,) * (mk.ndim - 2) + (n_mem, X_W)
    const = lambda b, i: (0, 0)
    return pl.pallas_call(
        _cross_kernel,
        grid=(nb, rows // tm),
        in_specs=[
            pl.BlockSpec((None, tm, d), lambda b, i: (b, i, 0)),
            _resident((1, d), const),
            _resident(wq.shape, const),
            pl.BlockSpec(kv_block, lambda b, i: kv_index(b)),
            pl.BlockSpec(kv_block, lambda b, i: kv_index(b)),
            _resident(wo.shape, const),
        ],
        out_specs=pl.BlockSpec((None, tm, d), lambda b, i: (b, i, 0)),
        out_shape=jax.ShapeDtypeStruct((nb, rows, d), F32),
        compiler_params=_params(2),
        name="cross",
    )(x3, g.reshape(1, d), wq, mk, mv, wo)


def _ffn_kernel(*refs, nb, tm, d_ff, final):
    if final:
        (x_ref, g_ref, state_ref, wup_ref, wconv_ref, bconv_ref, wdown_ref, gfin_ref,
         o_ref, ctail_ref, carry, ebuf, act) = refs
    else:
        (x_ref, g_ref, state_ref, wup_ref, wconv_ref, bconv_ref, wdown_ref,
         o_ref, ctail_ref, carry, ebuf, act) = refs
    i = pl.program_id(0)
    d = x_ref.shape[-1]
    hist = CONV_W - 1
    top = 8

    @pl.when(i == 0)
    def _():
        carry[:, top - hist:top, :] = state_ref[...]

    x = x_ref[...].reshape(nb * tm, d)
    h = _rmsnorm(x, g_ref[...]).astype(BF16)
    for j in range(d_ff // FF_CHUNK):
        c_half = []
        for half in range(2):
            c0 = half * d_ff + j * FF_CHUNK
            u = _dot(h, wup_ref[:, c0:c0 + FF_CHUNK])
            ebuf[half, :, top - hist:top, :] = carry[:, top - hist:top, c0:c0 + FF_CHUNK]
            ebuf[half, :, top:top + tm, :] = u.reshape(nb, tm, FF_CHUNK)
            carry[:, top - hist:top, c0:c0 + FF_CHUNK] = ebuf[half, :, top + tm - hist:top + tm, :]
            c = bconv_ref[:, c0:c0 + FF_CHUNK]
            for k in range(CONV_W):
                c = c + ebuf[half, :, top - hist + k:top - hist + k + tm, :] * wconv_ref[k:k + 1, c0:c0 + FF_CHUNK]
            c_half.append(c)
        a = jax.nn.silu(c_half[0]) * c_half[1]
        act[:, j * FF_CHUNK:(j + 1) * FF_CHUNK] = a.reshape(nb * tm, FF_CHUNK).astype(BF16)
    out = x + _dot(act[...], wdown_ref[...])
    if final:
        out = _rmsnorm(out, gfin_ref[...])
    o_ref[...] = out.reshape(nb, tm, d)

    @pl.when(i == pl.num_programs(0) - 1)
    def _():
        ctail_ref[...] = carry[:, top - hist:top, :]


def _ffn(x3, g, state, wup, wconv, bconv, wdown, gfin, tm):
    nb, rows, d = x3.shape
    d_ff = wdown.shape[0]
    final = gfin is not None
    const = lambda i: (0, 0)
    in_specs = [
        pl.BlockSpec((nb, tm, d), lambda i: (0, i, 0)),
        _resident((1, d), const),
        _resident(state.shape, lambda i: (0, 0, 0)),
        _resident(wup.shape, const),
        _resident(wconv.shape, const),
        _resident((1, 2 * d_ff), const),
        _resident(wdown.shape, const),
    ]
    args = [x3, g.reshape(1, d), state, wup, wconv, bconv.reshape(1, 2 * d_ff), wdown]
    if final:
        in_specs.append(_resident((1, d), const))
        args.append(gfin.reshape(1, d))
    return pl.pallas_call(
        functools.partial(_ffn_kernel, nb=nb, tm=tm, d_ff=d_ff, final=final),
        grid=(rows // tm,),
        in_specs=in_specs,
        out_specs=[
            pl.BlockSpec((nb, tm, d), lambda i: (0, i, 0)),
            pl.BlockSpec((nb, CONV_W - 1, 2 * d_ff), lambda i: (0, 0, 0)),
        ],
        out_shape=[
            jax.ShapeDtypeStruct((nb, rows, d), F32),
            jax.ShapeDtypeStruct((nb, CONV_W - 1, 2 * d_ff), F32),
        ],
        scratch_shapes=[
            pltpu.VMEM((nb, 8, 2 * d_ff), F32),
            pltpu.VMEM((2, nb, 8 + tm, FF_CHUNK), F32),
            pltpu.VMEM((nb * tm, d_ff), BF16),
        ],
        compiler_params=_params(),
        name="conv_ffn",
    )(*args)


def _alibi_slopes():
    return np.exp2(-8.0 * np.arange(1, A_HEADS + 1) / A_HEADS).astype(np.float32)


def _pair_major(w, axis):
    shp = w.shape
    w = w.reshape(shp[:axis] + (A_KV_HEADS, A_GROUP, HEAD_DIM) + shp[axis + 1:])
    w = jnp.swapaxes(w, axis, axis + 1)
    return w.reshape(shp)


def _prep_w_in(w):
    d = w.shape[1]
    c = np.cumsum([0, A_Q, A_KV, A_KV, B_W, B_W, B_W, d, d])
    qa, ka, va, qb, kb, vb, ga, gb = [w[..., c[i]:c[i + 1]] for i in range(8)]
    qscale = HEAD_DIM ** -0.5 * LOG2E
    qa = _pair_major(qa, 2) * qscale
    return jnp.concatenate([ga, gb, qa, qb * qscale, kb, vb, ka, va], axis=-1).astype(BF16)


def _band_valid(back, band, tm):
    npair = tm // PAIR
    r = np.arange(PAIR)[:, None]
    c = np.arange(band)[None, :]
    qc, kc = r // CHUNK, c // CHUNK
    in_band = (kc >= qc) & (kc <= qc + back)
    history = band - PAIR
    variants = [in_band & (j * PAIR + c >= history) for j in range(npair)] + [in_band]
    return np.where(np.stack(variants), 0.0, NEG).astype(np.float32)


def _prompt_bias_a(tm):
    r = np.arange(PAIR)[:, None]
    c = np.arange(A_BAND)[None, :]
    dist = np.abs(A_BACK * CHUNK + r - c).astype(np.float32)
    slopes = _alibi_slopes().reshape(1, A_KV_HEADS, A_GROUP, 1, 1)
    bias = (-slopes * dist + _band_valid(A_BACK, A_BAND, tm)[:, None, None]) * np.float32(LOG2E)
    return jnp.asarray(bias.reshape(-1, A_KV_HEADS, A_GROUP * PAIR, A_BAND))


def _take_clipped(table, idx):
    idx = np.asarray(idx)
    n_lo = int(np.sum(idx == idx[0])) - 1
    n_hi = int(np.sum(idx == idx[-1])) - 1 if idx[-1] != idx[0] else 0
    want = np.concatenate([np.full(n_lo, idx[0]), np.arange(idx[0], idx[-1] + 1), np.full(n_hi, idx[-1])])
    assert np.array_equal(want, idx)
    lead = table.shape[:-1]
    parts = [jnp.broadcast_to(table[..., idx[0]:idx[0] + 1], lead + (n_lo,)),
             table[..., idx[0]:idx[-1] + 1],
             jnp.broadcast_to(table[..., idx[-1]:idx[-1] + 1], lead + (n_hi,))]
    return jnp.concatenate(parts, axis=-1)


def _rel_diagonals(table, n_q, offset, width):
    idx = np.clip(np.arange(width) - (n_q - 1) + offset, -REL_CLIP, REL_CLIP) + REL_CLIP
    return _take_clipped(table, idx)


def _rel_bias_kernel(ext_ref, valid_ref, out_ref):
    w = ext_ref.shape[-1]
    for h in range(B_HEADS):
        x = jnp.broadcast_to(ext_ref[h:h + 1, :], (PAIR, w))
        y = pltpu.roll(x, w - (PAIR - 1), 1, stride=1, stride_axis=0)[:, :B_BAND]
        for j in range(valid_ref.shape[0]):
            out_ref[j, h] = y + valid_ref[j]


def _prompt_bias_b(tables, tm):
    depth = tables.shape[0]
    w = 768
    assert w % 128 == 0 and w >= PAIR + B_BAND - 1
    ext = _rel_diagonals(tables, PAIR, -B_BACK * CHUNK, w)
    valid = jnp.asarray(_band_valid(B_BACK, B_BAND, tm) * np.float32(LOG2E))
    nvar = valid.shape[0]
    return pl.pallas_call(
        _rel_bias_kernel,
        grid=(depth,),
        in_specs=[
            pl.BlockSpec((None, B_HEADS, w), lambda l: (l, 0, 0)),
            pl.BlockSpec(valid.shape, lambda l: (0, 0, 0)),
        ],
        out_specs=pl.BlockSpec((None, nvar, B_HEADS, PAIR, B_BAND), lambda l: (l, 0, 0, 0, 0)),
        out_shape=jax.ShapeDtypeStruct((depth, nvar, B_HEADS, PAIR, B_BAND), F32),
        compiler_params=_params(),
        name="rel_bias",
    )(ext, valid)


def _chunk_valid(pos_q, pos_k, back):
    cq, ck = pos_q[:, None] // CHUNK, pos_k[None, :] // CHUNK
    return np.where((ck <= cq) & (ck >= cq - back), 0.0, NEG).astype(np.float32)


def _sample_bias_a(past_len, la, tn):
    pos_q = past_len + np.arange(tn)
    slopes = _alibi_slopes().reshape(A_KV_HEADS, A_GROUP, 1, 1)

    def bias(pos_k):
        dist = np.abs(pos_q[:, None] - pos_k[None, :]).astype(np.float32)
        b = (-slopes * dist + _chunk_valid(pos_q, pos_k, A_BACK)) * np.float32(LOG2E)
        return jnp.asarray(b.reshape(A_KV_HEADS, A_GROUP * tn, pos_k.shape[0]))

    return bias(past_len - la + np.arange(la)), bias(pos_q)


def _sample_bias_b(tables, past_len, lb, tn):
    pos_q = past_len + np.arange(tn)

    def bias(pos_k, offset):
        n_k = pos_k.shape[0]
        ext = _rel_diagonals(tables, tn, offset, n_k + tn - 1)
        rows = [ext[..., tn - 1 - i:tn - 1 - i + n_k] for i in range(tn)]
        valid = _chunk_valid(pos_q, pos_k, B_BACK) * np.float32(LOG2E)
        return jnp.stack(rows, axis=-2) + valid

    return bias(past_len - lb + np.arange(lb), -lb), bias(pos_q, 0)


def _sink_rows(sink, rows):
    s = jnp.repeat(sink.reshape(A_KV_HEADS, A_GROUP, 1) * LOG2E, rows, axis=2)
    return s.reshape(A_KV_HEADS, A_GROUP * rows, 1)


def kernel(x_prompt, x_sample, mem_prompt, cache_a_k, cache_a_v, cache_b_k, cache_b_v, cache_mem_k, cache_mem_v, state_conv, g_mix, w_mix_in, a_sink, b_rel_bias, w_o_a, w_o_b, w_mix_out, g_xattn, g_mem, w_xq, w_xk, w_xv, w_xo, g_ffn, w_up, w_conv, b_conv, w_down, g_final):
    batch, seq, d = x_prompt.shape
    dec_batch, dec_seq, _ = x_sample.shape
    depth = w_mix_in.shape[0]
    n_mem = mem_prompt.shape[1]
    d_ff = w_down.shape[1]
    la, lb = cache_a_k.shape[2], cache_b_k.shape[2]
    past_len = PAST_LEN
    assert batch == 1 and seq % ROW_TILE == 0 and d_ff % FF_CHUNK == 0
    na, nbk = min(A_BACK * CHUNK, seq), min(B_BACK * CHUNK, seq)
    assert nbk == ROW_TILE and na <= ROW_TILE
    ts = dec_batch * dec_seq

    w_in = _prep_w_in(w_mix_in)
    woa = _pair_major(w_o_a, 1).astype(BF16)
    wob = w_o_b.astype(BF16)
    wout = w_mix_out.astype(BF16)
    wxq, wxk, wxv, wxo = (w.astype(BF16) for w in (w_xq, w_xk, w_xv, w_xo))
    wup = w_up.astype(BF16)
    wdown = w_down.astype(BF16)

    rel_tables = b_rel_bias.astype(F32) * LOG2E
    bias_a_p = _prompt_bias_a(ROW_TILE)
    bias_b_p = _prompt_bias_b(rel_tables, ROW_TILE)
    bias_a_sc, bias_a_sn = _sample_bias_a(past_len, la, dec_seq)
    bias_b_sc, bias_b_sn = _sample_bias_b(rel_tables, past_len, lb, dec_seq)
    cak = cache_a_k.reshape(depth, dec_batch, la, A_KV)
    cav = cache_a_v.reshape(depth, dec_batch, la, A_KV)
    cbk = cache_b_k.reshape(depth, dec_batch, lb, B_W)
    cbv = cache_b_v.reshape(depth, dec_batch, lb, B_W)
    cmk = cache_mem_k.reshape(depth, dec_batch, n_mem, X_W)
    cmv = cache_mem_v.reshape(depth, dec_batch, n_mem, X_W)
    zero_state = jnp.zeros((batch, CONV_W - 1, 2 * d_ff), F32)

    pm_k, pm_v = _memkv(mem_prompt[0], g_mem, wxk, wxv)

    xp = x_prompt.reshape(seq, d)
    xs = x_sample.reshape(ts, d)
    outs = {k: [] for k in ("pa_k", "pa_v", "pb_k", "pb_v", "pconv", "sa_k", "sa_v", "sb_k", "sb_v", "sconv")}
    for l in range(depth):
        gfin = g_final if l == depth - 1 else None
        qkv, gate, tail = _inproj(xp, g_mix[l], w_in[l], ROW_TILE)
        oa, ob = _attn_prompt(qkv, bias_a_p, bias_b_p, l, _sink_rows(a_sink[l], PAIR))
        xp = _merge(xp, oa, ob, gate, woa[l], wob[l], wout[l], ROW_TILE)
        xp = _cross(xp.reshape(batch, seq, d), g_xattn[l], wxq[l], pm_k, pm_v, wxo[l], ROW_TILE,
                    lambda b, l=l: (l, 0, 0))
        xp, cp = _ffn(xp, g_ffn[l], zero_state, wup[l], w_conv[l], b_conv[l], wdown[l], gfin, ROW_TILE)
        xp = xp.reshape(seq, d)
        outs["pb_k"].append(tail[ROW_TILE - nbk:, 0:B_W].reshape(batch, nbk, B_HEADS, HEAD_DIM))
        outs["pb_v"].append(tail[ROW_TILE - nbk:, B_W:2 * B_W].reshape(batch, nbk, B_HEADS, HEAD_DIM))
        outs["pa_k"].append(tail[ROW_TILE - na:, 2 * B_W:2 * B_W + A_KV].reshape(batch, na, A_KV_HEADS, HEAD_DIM))
        outs["pa_v"].append(tail[ROW_TILE - na:, 2 * B_W + A_KV:].reshape(batch, na, A_KV_HEADS, HEAD_DIM))
        outs["pconv"].append(cp)
        qkv, gate, tail = _inproj(xs, g_mix[l], w_in[l], ts)
        oa, ob = _attn_sample(qkv, cak, cav, cbk, cbv, l, dec_seq, bias_a_sc, bias_a_sn,
                              bias_b_sc[l], bias_b_sn[l], _sink_rows(a_sink[l], dec_seq))
        xs = _merge(xs, oa, ob, gate, woa[l], wob[l], wout[l], ts)
        xs = _cross(xs.reshape(dec_batch, dec_seq, d), g_xattn[l], wxq[l], cmk, cmv, wxo[l], dec_seq,
                    lambda b, l=l: (l, b, 0, 0))
        xs, cs = _ffn(xs, g_ffn[l], state_conv[l], wup[l], w_conv[l], b_conv[l], wdown[l], gfin, dec_seq)
        xs = xs.reshape(ts, d)
        outs["sb_k"].append(tail[:, 0:B_W].reshape(dec_batch, dec_seq, B_HEADS, HEAD_DIM))
        outs["sb_v"].append(tail[:, B_W:2 * B_W].reshape(dec_batch, dec_seq, B_HEADS, HEAD_DIM))
        outs["sa_k"].append(tail[:, 2 * B_W:2 * B_W + A_KV].reshape(dec_batch, dec_seq, A_KV_HEADS, HEAD_DIM))
        outs["sa_v"].append(tail[:, 2 * B_W + A_KV:].reshape(dec_batch, dec_seq, A_KV_HEADS, HEAD_DIM))
        outs["sconv"].append(cs)

    st = {k: jnp.stack(v) for k, v in outs.items()}
    pm_k = pm_k.reshape(depth, batch, n_mem, X_HEADS, X_HEAD_DIM)
    pm_v = pm_v.reshape(depth, batch, n_mem, X_HEADS, X_HEAD_DIM)
    return (xp.reshape(batch, seq, d), xs.reshape(dec_batch, dec_seq, d),
            st["pa_k"], st["pa_v"], st["pb_k"], st["pb_v"], pm_k, pm_v, st["pconv"],
            st["sa_k"], st["sa_v"], st["sb_k"], st["sb_v"], st["sconv"])
```

```python
import functools

import numpy as np
import jax
import jax.numpy as jnp
from jax import lax
from jax.experimental import pallas as pl
from jax.experimental.pallas import tpu as pltpu

CHUNK = 64
HEAD_DIM = 64
A_HEADS = 8
A_KV_HEADS = 2
A_GROUP = A_HEADS // A_KV_HEADS
A_BACK = 2
B_HEADS = 8
B_BACK = 8
REL_CLIP = 256
X_HEADS = 4
X_HEAD_DIM = 128
EPS = 1e-6
NEG = -1e30
LOG2E = float(np.log2(np.e))

A_Q = A_HEADS * HEAD_DIM
A_KV = A_KV_HEADS * HEAD_DIM
B_W = B_HEADS * HEAD_DIM
X_W = X_HEADS * X_HEAD_DIM
CONV_W = 3
PAST_LEN = 1024

PAIR = 2 * CHUNK
ROW_TILE = 512
A_BAND = (A_BACK + 2) * CHUNK
B_BAND = (B_BACK + 2) * CHUNK
FF_CHUNK = 256
QK_AHEAD = 2

V7X_VMEM_LIMIT = 56 * 1024 * 1024

F32 = jnp.float32
BF16 = jnp.bfloat16


def _params(n_axes=1):
    return pltpu.CompilerParams(
        dimension_semantics=("arbitrary",) * n_axes,
        vmem_limit_bytes=V7X_VMEM_LIMIT)


def _resident(shape, index_map):
    return pl.BlockSpec(shape, index_map, pipeline_mode=pl.Buffered(1))


def _rmsnorm(x, g):
    ms = jnp.mean(x * x, axis=-1, keepdims=True)
    return (x * lax.rsqrt(ms + EPS)) * g


def _dot(a, b):
    return jnp.dot(a, b, preferred_element_type=F32)


def _dot_nt(a, b):
    return lax.dot_general(a, b, (((1,), (1,)), ((), ())), preferred_element_type=F32)


def _softmax_pv(s_list, v_list, sink=None):
    m = s_list[0].max(axis=-1, keepdims=True)
    for s in s_list[1:]:
        m = jnp.maximum(m, s.max(axis=-1, keepdims=True))
    if sink is not None:
        m = jnp.maximum(m, sink)
    l = None
    o = None
    for s, v in zip(s_list, v_list):
        p = jnp.exp2(s - m)
        ls = p.sum(axis=-1, keepdims=True)
        os_ = _dot(p.astype(BF16), v)
        l = ls if l is None else l + ls
        o = os_ if o is None else o + os_
    if sink is not None:
        l = l + jnp.exp2(sink - m)
    return o * (1.0 / l)


def _lane_lo(shape):
    return lax.broadcasted_iota(jnp.int32, shape, len(shape) - 1) < HEAD_DIM


def _memkv_kernel(mem_ref, g_ref, wk_ref, wv_ref, k_ref, v_ref):
    hm = _rmsnorm(mem_ref[...], g_ref[...]).astype(BF16)
    k_ref[...] = _dot(hm, wk_ref[...])
    v_ref[...] = _dot(hm, wv_ref[...])


def _memkv(mem, g_mem, w_xk, w_xv):
    depth, d, xw = w_xk.shape
    n_mem = mem.shape[0]
    return pl.pallas_call(
        _memkv_kernel,
        grid=(depth,),
        in_specs=[
            pl.BlockSpec((n_mem, d), lambda l: (0, 0)),
            pl.BlockSpec((None, 1, d), lambda l: (l, 0, 0)),
            pl.BlockSpec((None, d, xw), lambda l: (l, 0, 0)),
            pl.BlockSpec((None, d, xw), lambda l: (l, 0, 0)),
        ],
        out_specs=[
            pl.BlockSpec((None, n_mem, xw), lambda l: (l, 0, 0)),
            pl.BlockSpec((None, n_mem, xw), lambda l: (l, 0, 0)),
        ],
        out_shape=[jax.ShapeDtypeStruct((depth, n_mem, xw), F32)] * 2,
        compiler_params=_params(),
        name="mem_kv",
    )(mem, g_mem.reshape(depth, 1, d), w_xk, w_xv)


N_GATE = 2048
N_QKV = A_Q + 3 * B_W + 2 * A_KV
N_TAIL = 2 * B_W + 2 * A_KV


def _inproj_kernel(x_ref, g_ref, w_ref, qkv_ref, gate_ref, tail_ref):
    h = _rmsnorm(x_ref[...], g_ref[...]).astype(BF16)
    gate_ref[...] = jax.nn.sigmoid(_dot(h, w_ref[:, :N_GATE])).astype(BF16)
    z = _dot(h, w_ref[:, N_GATE:])
    qkv_ref[...] = z.astype(BF16)

    @pl.when(pl.program_id(0) == pl.num_programs(0) - 1)
    def _():
        tail_ref[...] = z[:, N_QKV - N_TAIL:]


def _inproj(x, g, w, tm):
    t, d = x.shape
    n = w.shape[1]
    return pl.pallas_call(
        _inproj_kernel,
        grid=(t // tm,),
        in_specs=[
            pl.BlockSpec((tm, d), lambda i: (i, 0)),
            _resident((1, d), lambda i: (0, 0)),
            _resident((d, n), lambda i: (0, 0)),
        ],
        out_specs=[
            pl.BlockSpec((tm, N_QKV), lambda i: (i, 0)),
            pl.BlockSpec((tm, N_GATE), lambda i: (i, 0)),
            pl.BlockSpec((tm, N_TAIL), lambda i: (0, 0)),
        ],
        out_shape=[
            jax.ShapeDtypeStruct((t, N_QKV), BF16),
            jax.ShapeDtypeStruct((t, N_GATE), BF16),
            jax.ShapeDtypeStruct((tm, N_TAIL), F32),
        ],
        compiler_params=_params(),
        name="inproj",
    )(x, g.reshape(1, d), w)


def _attn_prompt_kernel(qa_ref, qb_ref, kbc_ref, kbp_ref, vbc_ref, vbp_ref,
                        kvac_ref, kvap_ref, bias_a_ref, bias_b_ref, sink_ref,
                        oa_ref, ob_ref, kcat, vcat, kvacat):
    tm = qa_ref.shape[0]
    npair = tm // PAIR
    first_tile = pl.program_id(0) == 0
    kcat[0:tm, :] = kbp_ref[...]
    kcat[tm:2 * tm, :] = kbc_ref[...]
    vcat[0:tm, :] = vbp_ref[...]
    vcat[tm:2 * tm, :] = vbc_ref[...]
    kvacat[0:PAIR, :] = kvap_ref[...]
    kvacat[PAIR:PAIR + tm, :] = kvac_ref[...]
    lo = _lane_lo((PAIR, 2 * HEAD_DIM))
    lo_a = _lane_lo((A_GROUP * PAIR, 2 * HEAD_DIM))

    def pair_body(d, carry):
        r0 = pl.multiple_of(d * PAIR, PAIR)
        var = jnp.where(first_tile, d, npair)
        ka2 = kvacat[pl.ds(r0, A_BAND), 0:A_KV]
        va2 = kvacat[pl.ds(r0, A_BAND), A_KV:2 * A_KV]
        qa = qa_ref[pl.ds(r0, PAIR), :]
        q_stack = jnp.concatenate(
            [qa[:, p * 2 * HEAD_DIM:(p + 1) * 2 * HEAD_DIM] for p in range(A_GROUP)], axis=0)
        b_operands = {}

        def b_group(hp):
            if hp not in b_operands:
                c0 = hp * 2 * HEAD_DIM
                b_operands[hp] = (qb_ref[pl.ds(r0, PAIR), c0:c0 + 2 * HEAD_DIM],
                                  kcat[pl.ds(r0, B_BAND), c0:c0 + 2 * HEAD_DIM],
                                  vcat[pl.ds(r0, B_BAND), c0:c0 + 2 * HEAD_DIM])
            return b_operands[hp]

        def scores(stage):
            if stage[0] == "a":
                kv = stage[1]
                qm = jnp.where(lo_a, q_stack, 0) if kv == 0 else jnp.where(lo_a, 0, q_stack)
                return _dot_nt(qm, ka2) + bias_a_ref[var, kv]
            _, hp, half = stage
            q2, k2, _ = b_group(hp)
            qm = jnp.where(lo, q2, 0) if half == 0 else jnp.where(lo, 0, q2)
            return _dot_nt(qm, k2) + bias_b_ref[var, 2 * hp + half]

        stages = ([("a", kv) for kv in range(A_KV_HEADS)]
                  + [("b", hp, half) for hp in range(B_HEADS // 2) for half in range(2)])
        done = {}
        pending = [scores(st) for st in stages[:QK_AHEAD]]
        for n, stage in enumerate(stages):
            if n + QK_AHEAD < len(stages):
                pending.append(scores(stages[n + QK_AHEAD]))
            s = pending.pop(0)
            if stage[0] == "a":
                done[stage] = _softmax_pv([s], [va2], sink_ref[stage[1]])
                if stage[1] == A_KV_HEADS - 1:
                    for p in range(A_GROUP):
                        rows = slice(p * PAIR, (p + 1) * PAIR)
                        o_pair = jnp.where(lo, done[("a", 0)][rows], done[("a", 1)][rows])
                        oa_ref[pl.ds(r0, PAIR), p * 2 * HEAD_DIM:(p + 1) * 2 * HEAD_DIM] = o_pair.astype(BF16)
            else:
                _, hp, half = stage
                done[stage] = _softmax_pv([s], [b_group(hp)[2]])
                if half == 1:
                    c0 = hp * 2 * HEAD_DIM
                    ob_ref[pl.ds(r0, PAIR), c0:c0 + 2 * HEAD_DIM] = jnp.where(
                        lo, done[("b", hp, 0)], done[("b", hp, 1)]).astype(BF16)
        return carry

    lax.fori_loop(0, tm // PAIR, pair_body, 0)


def _attn_prompt(qkv, bias_a, bias_b, layer, sink_rows):
    t = qkv.shape[0]
    tm = ROW_TILE
    npair = tm // PAIR
    prev = lambda i: jnp.maximum(i - 1, 0)
    return pl.pallas_call(
        _attn_prompt_kernel,
        grid=(t // tm,),
        in_specs=[
            pl.BlockSpec((tm, A_Q), lambda i: (i, 0)),
            pl.BlockSpec((tm, B_W), lambda i: (i, 1)),
            pl.BlockSpec((tm, B_W), lambda i: (i, 2)),
            pl.BlockSpec((tm, B_W), lambda i: (prev(i), 2)),
            pl.BlockSpec((tm, B_W), lambda i: (i, 3)),
            pl.BlockSpec((tm, B_W), lambda i: (prev(i), 3)),
            pl.BlockSpec((tm, 2 * A_KV), lambda i: (i, (A_Q + 3 * B_W) // (2 * A_KV))),
            pl.BlockSpec((PAIR, 2 * A_KV),
                         lambda i: (jnp.maximum(i * npair - 1, 0), (A_Q + 3 * B_W) // (2 * A_KV))),
            _resident(bias_a.shape, lambda i: (0, 0, 0, 0)),
            _resident((None,) + bias_b.shape[1:], lambda i: (layer, 0, 0, 0, 0)),
            _resident(sink_rows.shape, lambda i: (0, 0, 0)),
        ],
        out_specs=[
            pl.BlockSpec((tm, A_Q), lambda i: (i, 0)),
            pl.BlockSpec((tm, B_W), lambda i: (i, 0)),
        ],
        out_shape=[jax.ShapeDtypeStruct((t, A_Q), BF16), jax.ShapeDtypeStruct((t, B_W), BF16)],
        scratch_shapes=[
            pltpu.VMEM((2 * tm, B_W), BF16),
            pltpu.VMEM((2 * tm, B_W), BF16),
            pltpu.VMEM((PAIR + tm, 2 * A_KV), BF16),
        ],
        compiler_params=_params(),
        name="attn_prompt",
    )(qkv, qkv, qkv, qkv, qkv, qkv, qkv, qkv, bias_a, bias_b, sink_rows)


def _attn_sample_kernel(qa_ref, qb_ref, kbn_ref, vbn_ref, kvan_ref,
                        cak_ref, cav_ref, cbk_ref, cbv_ref,
                        bias_ac_ref, bias_an_ref, bias_bc_ref, bias_bn_ref, sink_ref,
                        oa_ref, ob_ref):
    tn = qa_ref.shape[0]
    lo = _lane_lo((tn, 2 * HEAD_DIM))
    lo_a = _lane_lo((A_GROUP * tn, 2 * HEAD_DIM))
    kac = cak_ref[...].astype(BF16)
    vac = cav_ref[...].astype(BF16)
    kan = kvan_ref[:, 0:A_KV]
    van = kvan_ref[:, A_KV:2 * A_KV]
    qa = qa_ref[...]
    q_stack = jnp.concatenate(
        [qa[:, p * 2 * HEAD_DIM:(p + 1) * 2 * HEAD_DIM] for p in range(A_GROUP)], axis=0)
    o_kv = []
    for kv in range(A_KV_HEADS):
        qm = jnp.where(lo_a, q_stack, 0) if kv == 0 else jnp.where(lo_a, 0, q_stack)
        s_c = _dot_nt(qm, kac) + bias_ac_ref[kv]
        s_n = _dot_nt(qm, kan) + bias_an_ref[kv]
        o_kv.append(_softmax_pv([s_c, s_n], [vac, van], sink_ref[kv]))
    for p in range(A_GROUP):
        o_pair = jnp.where(lo, o_kv[0][p * tn:(p + 1) * tn], o_kv[1][p * tn:(p + 1) * tn])
        oa_ref[:, p * 2 * HEAD_DIM:(p + 1) * 2 * HEAD_DIM] = o_pair.astype(BF16)
    for hp in range(B_HEADS // 2):
        c0 = hp * 2 * HEAD_DIM
        q2 = qb_ref[:, c0:c0 + 2 * HEAD_DIM]
        kc = cbk_ref[:, c0:c0 + 2 * HEAD_DIM].astype(BF16)
        vc = cbv_ref[:, c0:c0 + 2 * HEAD_DIM].astype(BF16)
        kn = kbn_ref[:, c0:c0 + 2 * HEAD_DIM]
        vn = vbn_ref[:, c0:c0 + 2 * HEAD_DIM]
        o_h = []
        for half in range(2):
            qm = jnp.where(lo, q2, 0) if half == 0 else jnp.where(lo, 0, q2)
            s_c = _dot_nt(qm, kc) + bias_bc_ref[2 * hp + half]
            s_n = _dot_nt(qm, kn) + bias_bn_ref[2 * hp + half]
            o_h.append(_softmax_pv([s_c, s_n], [vc, vn]))
        ob_ref[:, c0:c0 + 2 * HEAD_DIM] = jnp.where(lo, o_h[0], o_h[1]).astype(BF16)


def _attn_sample(qkv, cak, cav, cbk, cbv, layer, tn, bias_ac, bias_an, bias_bc, bias_bn, sink_rows):
    t = qkv.shape[0]
    nb = t // tn
    la = cak.shape[2]
    lb = cbk.shape[2]
    kva_blk = (A_Q + 3 * B_W) // (2 * A_KV)
    const3 = lambda b: (0, 0, 0)
    return pl.pallas_call(
        _attn_sample_kernel,
        grid=(nb,),
        in_specs=[
            pl.BlockSpec((tn, A_Q), lambda b: (b, 0)),
            pl.BlockSpec((tn, B_W), lambda b: (b, 1)),
            pl.BlockSpec((tn, B_W), lambda b: (b, 2)),
            pl.BlockSpec((tn, B_W), lambda b: (b, 3)),
            pl.BlockSpec((tn, 2 * A_KV), lambda b: (b, kva_blk)),
            pl.BlockSpec((None, None, la, A_KV), lambda b: (layer, b, 0, 0)),
            pl.BlockSpec((None, None, la, A_KV), lambda b: (layer, b, 0, 0)),
            pl.BlockSpec((None, None, lb, B_W), lambda b: (layer, b, 0, 0)),
            pl.BlockSpec((None, None, lb, B_W), lambda b: (layer, b, 0, 0)),
            _resident(bias_ac.shape, const3),
            _resident(bias_an.shape, const3),
            _resident(bias_bc.shape, const3),
            _resident(bias_bn.shape, const3),
            _resident(sink_rows.shape, const3),
        ],
        out_specs=[
            pl.BlockSpec((tn, A_Q), lambda b: (b, 0)),
            pl.BlockSpec((tn, B_W), lambda b: (b, 0)),
        ],
        out_shape=[jax.ShapeDtypeStruct((t, A_Q), BF16), jax.ShapeDtypeStruct((t, B_W), BF16)],
        compiler_params=_params(),
        name="attn_sample",
    )(qkv, qkv, qkv, qkv, qkv, cak, cav, cbk, cbv, bias_ac, bias_an, bias_bc, bias_bn, sink_rows)


def _merge_rows(x, oa_ref, ob_ref, gate_ref, woa_ref, wob_ref, wout_ref):
    d = x.shape[1]
    a = _dot(oa_ref[...], woa_ref[...])
    b = _dot(ob_ref[...], wob_ref[...])
    m = gate_ref[:, 0:d].astype(F32) * a + gate_ref[:, d:2 * d].astype(F32) * b
    return x + _dot(m.astype(BF16), wout_ref[...])


def _merge_kernel(x_ref, oa_ref, ob_ref, gate_ref, woa_ref, wob_ref, wout_ref, o_ref):
    o_ref[...] = _merge_rows(x_ref[...], oa_ref, ob_ref, gate_ref, woa_ref, wob_ref, wout_ref)


def _merge(x, oa, ob, gate, woa, wob, wout, tm):
    t, d = x.shape
    row = lambda i: (i, 0)
    const = lambda i: (0, 0)
    return pl.pallas_call(
        _merge_kernel,
        grid=(t // tm,),
        in_specs=[
            pl.BlockSpec((tm, d), row),
            pl.BlockSpec((tm, A_Q), row),
            pl.BlockSpec((tm, B_W), row),
            pl.BlockSpec((tm, 2 * d), row),
            _resident(woa.shape, const),
            _resident(wob.shape, const),
            _resident(wout.shape, const),
        ],
        out_specs=pl.BlockSpec((tm, d), row),
        out_shape=jax.ShapeDtypeStruct((t, d), F32),
        compiler_params=_params(),
        name="merge",
    )(x, oa, ob, gate, woa, wob, wout)


def _cross_rows(x, g_ref, wq_ref, mk_ref, mv_ref, wo_ref):
    h = _rmsnorm(x, g_ref[...]).astype(BF16)
    q = _dot(h, wq_ref[...]).astype(BF16)
    mk = mk_ref[...].astype(BF16)
    mv = mv_ref[...].astype(BF16)
    scale = X_HEAD_DIM ** -0.5 * LOG2E

    def scores(hd):
        c0 = hd * X_HEAD_DIM
        return _dot_nt(q[:, c0:c0 + X_HEAD_DIM], mk[:, c0:c0 + X_HEAD_DIM]) * scale

    pending = [scores(hd) for hd in range(min(QK_AHEAD, X_HEADS))]
    outs = []
    for hd in range(X_HEADS):
        if hd + QK_AHEAD < X_HEADS:
            pending.append(scores(hd + QK_AHEAD))
        c0 = hd * X_HEAD_DIM
        outs.append(_softmax_pv([pending.pop(0)], [mv[:, c0:c0 + X_HEAD_DIM]]))
    o = jnp.concatenate(outs, axis=-1).astype(BF16)
    return x + _dot(o, wo_ref[...])


def _cross_kernel(x_ref, g_ref, wq_ref, mk_ref, mv_ref, wo_ref, o_ref):
    o_ref[...] = _cross_rows(x_ref[...], g_ref, wq_ref, mk_ref, mv_ref, wo_ref)


def _mixout_kernel(x_ref, oa_ref, ob_ref, gate_ref, woa_ref, wob_ref, wout_ref,
                   g_ref, wq_ref, mk_ref, mv_ref, wo_ref, o_ref):
    x = _merge_rows(x_ref[...], oa_ref, ob_ref, gate_ref, woa_ref, wob_ref, wout_ref)
    o_ref[...] = _cross_rows(x, g_ref, wq_ref, mk_ref, mv_ref, wo_ref)


def _mixout(x, oa, ob, gate, woa, wob, wout, g, wq, mk, mv, layer, wo, tm):
    t, d = x.shape
    n_mem = mk.shape[-2]
    row = lambda i: (i, 0)
    const = lambda i: (0, 0)
    return pl.pallas_call(
        _mixout_kernel,
        grid=(t // tm,),
        in_specs=[
            pl.BlockSpec((tm, d), row),
            pl.BlockSpec((tm, A_Q), row),
            pl.BlockSpec((tm, B_W), row),
            pl.BlockSpec((tm, 2 * d), row),
            _resident(woa.shape, const),
            _resident(wob.shape, const),
            _resident(wout.shape, const),
            _resident((1, d), const),
            _resident(wq.shape, const),
            _resident((None, n_mem, X_W), lambda i: (layer, 0, 0)),
            _resident((None, n_mem, X_W), lambda i: (layer, 0, 0)),
            _resident(wo.shape, const),
        ],
        out_specs=pl.BlockSpec((tm, d), row),
        out_shape=jax.ShapeDtypeStruct((t, d), F32),
        compiler_params=_params(),
        name="mix_out",
    )(x, oa, ob, gate, woa, wob, wout, g.reshape(1, d), wq, mk, mv, wo)


def _cross(x3, g, wq, mk, mv, wo, tm, kv_index):
    nb, rows, d = x3.shape
    n_mem = mk.shape[-2]
    kv_block = (None,) * (mk.ndim - 2) + (n_mem, X_W)
    const = lambda b, i: (0, 0)
    return pl.pallas_call(
        _cross_kernel,
        grid=(nb, rows // tm),
        in_specs=[
            pl.BlockSpec((None, tm, d), lambda b, i: (b, i, 0)),
            _resident((1, d), const),
            _resident(wq.shape, const),
            pl.BlockSpec(kv_block, lambda b, i: kv_index(b)),
            pl.BlockSpec(kv_block, lambda b, i: kv_index(b)),
            _resident(wo.shape, const),
        ],
        out_specs=pl.BlockSpec((None, tm, d), lambda b, i: (b, i, 0)),
        out_shape=jax.ShapeDtypeStruct((nb, rows, d), F32),
        compiler_params=_params(2),
        name="cross",
    )(x3, g.reshape(1, d), wq, mk, mv, wo)


def _ffn_kernel(*refs, nb, tm, d_ff, final):
    if final:
        (x_ref, g_ref, state_ref, wup_ref, wconv_ref, bconv_ref, wdown_ref, gfin_ref,
         o_ref, ctail_ref, carry, ebuf, act) = refs
    else:
        (x_ref, g_ref, state_ref, wup_ref, wconv_ref, bconv_ref, wdown_ref,
         o_ref, ctail_ref, carry, ebuf, act) = refs
    i = pl.program_id(0)
    d = x_ref.shape[-1]
    hist = CONV_W - 1
    top = 8

    @pl.when(i == 0)
    def _():
        carry[:, top - hist:top, :] = state_ref[...]

    x = x_ref[...].reshape(nb * tm, d)
    h = _rmsnorm(x, g_ref[...]).astype(BF16)
    for j in range(d_ff // FF_CHUNK):
        c_half = []
        for half in range(2):
            c0 = half * d_ff + j * FF_CHUNK
            u = _dot(h, wup_ref[:, c0:c0 + FF_CHUNK])
            ebuf[half, :, top - hist:top, :] = carry[:, top - hist:top, c0:c0 + FF_CHUNK]
            ebuf[half, :, top:top + tm, :] = u.reshape(nb, tm, FF_CHUNK)
            carry[:, top - hist:top, c0:c0 + FF_CHUNK] = ebuf[half, :, top + tm - hist:top + tm, :]
            c = bconv_ref[:, c0:c0 + FF_CHUNK]
            for k in range(CONV_W):
                c = c + ebuf[half, :, top - hist + k:top - hist + k + tm, :] * wconv_ref[k:k + 1, c0:c0 + FF_CHUNK]
            c_half.append(c)
        a = jax.nn.silu(c_half[0]) * c_half[1]
        act[:, j * FF_CHUNK:(j + 1) * FF_CHUNK] = a.reshape(nb * tm, FF_CHUNK).astype(BF16)
    out = x + _dot(act[...], wdown_ref[...])
    if final:
        out = _rmsnorm(out, gfin_ref[...])
    o_ref[...] = out.reshape(nb, tm, d)

    @pl.when(i == pl.num_programs(0) - 1)
    def _():
        ctail_ref[...] = carry[:, top - hist:top, :]


def _ffn(x3, g, state, wup, wconv, bconv, wdown, gfin, tm):
    nb, rows, d = x3.shape
    d_ff = wdown.shape[0]
    final = gfin is not None
    const = lambda i: (0, 0)
    in_specs = [
        pl.BlockSpec((nb, tm, d), lambda i: (0, i, 0)),
        _resident((1, d), const),
        _resident(state.shape, lambda i: (0, 0, 0)),
        _resident(wup.shape, const),
        _resident(wconv.shape, const),
        _resident((1, 2 * d_ff), const),
        _resident(wdown.shape, const),
    ]
    args = [x3, g.reshape(1, d), state, wup, wconv, bconv.reshape(1, 2 * d_ff), wdown]
    if final:
        in_specs.append(_resident((1, d), const))
        args.append(gfin.reshape(1, d))
    return pl.pallas_call(
        functools.partial(_ffn_kernel, nb=nb, tm=tm, d_ff=d_ff, final=final),
        grid=(rows // tm,),
        in_specs=in_specs,
        out_specs=[
            pl.BlockSpec((nb, tm, d), lambda i: (0, i, 0)),
            pl.BlockSpec((nb, CONV_W - 1, 2 * d_ff), lambda i: (0, 0, 0)),
        ],
        out_shape=[
            jax.ShapeDtypeStruct((nb, rows, d), F32),
            jax.ShapeDtypeStruct((nb, CONV_W - 1, 2 * d_ff), F32),
        ],
        scratch_shapes=[
            pltpu.VMEM((nb, 8, 2 * d_ff), F32),
            pltpu.VMEM((2, nb, 8 + tm, FF_CHUNK), F32),
            pltpu.VMEM((nb * tm, d_ff), BF16),
        ],
        compiler_params=_params(),
        name="conv_ffn",
    )(*args)


def _alibi_slopes():
    return np.exp2(-8.0 * np.arange(1, A_HEADS + 1) / A_HEADS).astype(np.float32)


def _pair_major(w, axis):
    shp = w.shape
    w = w.reshape(shp[:axis] + (A_KV_HEADS, A_GROUP, HEAD_DIM) + shp[axis + 1:])
    w = jnp.swapaxes(w, axis, axis + 1)
    return w.reshape(shp)


def _prep_w_in(w):
    d = w.shape[1]
    c = np.cumsum([0, A_Q, A_KV, A_KV, B_W, B_W, B_W, d, d])
    qa, ka, va, qb, kb, vb, ga, gb = [w[..., c[i]:c[i + 1]] for i in range(8)]
    qscale = HEAD_DIM ** -0.5 * LOG2E
    qa = _pair_major(qa, 2) * qscale
    return jnp.concatenate([ga, gb, qa, qb * qscale, kb, vb, ka, va], axis=-1).astype(BF16)


def _band_valid(back, band, tm):
    npair = tm // PAIR
    r = np.arange(PAIR)[:, None]
    c = np.arange(band)[None, :]
    qc, kc = r // CHUNK, c // CHUNK
    in_band = (kc >= qc) & (kc <= qc + back)
    history = band - PAIR
    variants = [in_band & (j * PAIR + c >= history) for j in range(npair)] + [in_band]
    return np.where(np.stack(variants), 0.0, NEG).astype(np.float32)


def _prompt_bias_a(tm):
    r = np.arange(PAIR)[:, None]
    c = np.arange(A_BAND)[None, :]
    dist = np.abs(A_BACK * CHUNK + r - c).astype(np.float32)
    slopes = _alibi_slopes().reshape(1, A_KV_HEADS, A_GROUP, 1, 1)
    bias = (-slopes * dist + _band_valid(A_BACK, A_BAND, tm)[:, None, None]) * np.float32(LOG2E)
    return jnp.asarray(bias.reshape(-1, A_KV_HEADS, A_GROUP * PAIR, A_BAND))


def _take_clipped(table, idx):
    idx = np.asarray(idx)
    n_lo = int(np.sum(idx == idx[0])) - 1
    n_hi = int(np.sum(idx == idx[-1])) - 1 if idx[-1] != idx[0] else 0
    want = np.concatenate([np.full(n_lo, idx[0]), np.arange(idx[0], idx[-1] + 1), np.full(n_hi, idx[-1])])
    assert np.array_equal(want, idx)
    lead = table.shape[:-1]
    parts = [jnp.broadcast_to(table[..., idx[0]:idx[0] + 1], lead + (n_lo,)),
             table[..., idx[0]:idx[-1] + 1],
             jnp.broadcast_to(table[..., idx[-1]:idx[-1] + 1], lead + (n_hi,))]
    return jnp.concatenate(parts, axis=-1)


def _rel_diagonals(table, n_q, offset, width):
    idx = np.clip(np.arange(width) - (n_q - 1) + offset, -REL_CLIP, REL_CLIP) + REL_CLIP
    return _take_clipped(table, idx)


def _rel_bias_kernel(ext_ref, valid_ref, out_ref):
    w = ext_ref.shape[-1]
    for h in range(B_HEADS):
        x = jnp.broadcast_to(ext_ref[h:h + 1, :], (PAIR, w))
        y = pltpu.roll(x, w - (PAIR - 1), 1, stride=1, stride_axis=0)[:, :B_BAND]
        for j in range(valid_ref.shape[0]):
            out_ref[j, h] = y + valid_ref[j]


def _prompt_bias_b(tables, tm):
    depth = tables.shape[0]
    w = 768
    assert w % 128 == 0 and w >= PAIR + B_BAND - 1
    ext = _rel_diagonals(tables, PAIR, -B_BACK * CHUNK, w)
    valid = jnp.asarray(_band_valid(B_BACK, B_BAND, tm) * np.float32(LOG2E))
    nvar = valid.shape[0]
    return pl.pallas_call(
        _rel_bias_kernel,
        grid=(depth,),
        in_specs=[
            pl.BlockSpec((None, B_HEADS, w), lambda l: (l, 0, 0)),
            pl.BlockSpec(valid.shape, lambda l: (0, 0, 0)),
        ],
        out_specs=pl.BlockSpec((None, nvar, B_HEADS, PAIR, B_BAND), lambda l: (l, 0, 0, 0, 0)),
        out_shape=jax.ShapeDtypeStruct((depth, nvar, B_HEADS, PAIR, B_BAND), F32),
        compiler_params=_params(),
        name="rel_bias",
    )(ext, valid)


def _chunk_valid(pos_q, pos_k, back):
    cq, ck = pos_q[:, None] // CHUNK, pos_k[None, :] // CHUNK
    return np.where((ck <= cq) & (ck >= cq - back), 0.0, NEG).astype(np.float32)


def _sample_bias_a(past_len, la, tn):
    pos_q = past_len + np.arange(tn)
    slopes = _alibi_slopes().reshape(A_KV_HEADS, A_GROUP, 1, 1)

    def bias(pos_k):
        dist = np.abs(pos_q[:, None] - pos_k[None, :]).astype(np.float32)
        b = (-slopes * dist + _chunk_valid(pos_q, pos_k, A_BACK)) * np.float32(LOG2E)
        return jnp.asarray(b.reshape(A_KV_HEADS, A_GROUP * tn, pos_k.shape[0]))

    return bias(past_len - la + np.arange(la)), bias(pos_q)


def _sample_bias_b(tables, past_len, lb, tn):
    pos_q = past_len + np.arange(tn)

    def bias(pos_k, offset):
        n_k = pos_k.shape[0]
        ext = _rel_diagonals(tables, tn, offset, n_k + tn - 1)
        rows = [ext[..., tn - 1 - i:tn - 1 - i + n_k] for i in range(tn)]
        valid = _chunk_valid(pos_q, pos_k, B_BACK) * np.float32(LOG2E)
        return jnp.stack(rows, axis=-2) + valid

    return bias(past_len - lb + np.arange(lb), -lb), bias(pos_q, 0)


def _sink_rows(sink, rows):
    s = jnp.repeat(sink.reshape(A_KV_HEADS, A_GROUP, 1) * LOG2E, rows, axis=2)
    return s.reshape(A_KV_HEADS, A_GROUP * rows, 1)


def kernel(x_prompt, x_sample, mem_prompt, cache_a_k, cache_a_v, cache_b_k, cache_b_v, cache_mem_k, cache_mem_v, state_conv, g_mix, w_mix_in, a_sink, b_rel_bias, w_o_a, w_o_b, w_mix_out, g_xattn, g_mem, w_xq, w_xk, w_xv, w_xo, g_ffn, w_up, w_conv, b_conv, w_down, g_final):
    batch, seq, d = x_prompt.shape
    dec_batch, dec_seq, _ = x_sample.shape
    depth = w_mix_in.shape[0]
    n_mem = mem_prompt.shape[1]
    d_ff = w_down.shape[1]
    la, lb = cache_a_k.shape[2], cache_b_k.shape[2]
    past_len = PAST_LEN
    assert batch == 1 and seq % ROW_TILE == 0 and d_ff % FF_CHUNK == 0
    na, nbk = min(A_BACK * CHUNK, seq), min(B_BACK * CHUNK, seq)
    assert nbk == ROW_TILE and na <= ROW_TILE
    ts = dec_batch * dec_seq

    w_in = _prep_w_in(w_mix_in)
    woa = _pair_major(w_o_a, 1).astype(BF16)
    wob = w_o_b.astype(BF16)
    wout = w_mix_out.astype(BF16)
    wxq, wxk, wxv, wxo = (w.astype(BF16) for w in (w_xq, w_xk, w_xv, w_xo))
    wup = w_up.astype(BF16)
    wdown = w_down.astype(BF16)

    rel_tables = b_rel_bias.astype(F32) * LOG2E
    bias_a_p = _prompt_bias_a(ROW_TILE)
    bias_b_p = _prompt_bias_b(rel_tables, ROW_TILE)
    bias_a_sc, bias_a_sn = _sample_bias_a(past_len, la, dec_seq)
    bias_b_sc, bias_b_sn = _sample_bias_b(rel_tables, past_len, lb, dec_seq)
    cak = cache_a_k.reshape(depth, dec_batch, la, A_KV)
    cav = cache_a_v.reshape(depth, dec_batch, la, A_KV)
    cbk = cache_b_k.reshape(depth, dec_batch, lb, B_W)
    cbv = cache_b_v.reshape(depth, dec_batch, lb, B_W)
    cmk = cache_mem_k.reshape(depth, dec_batch, n_mem, X_W)
    cmv = cache_mem_v.reshape(depth, dec_batch, n_mem, X_W)
    zero_state = jnp.zeros((batch, CONV_W - 1, 2 * d_ff), F32)

    pm_k, pm_v = _memkv(mem_prompt[0], g_mem, wxk, wxv)
    pm_k_bf, pm_v_bf = pm_k.astype(BF16), pm_v.astype(BF16)

    xp = x_prompt.reshape(seq, d)
    xs = x_sample.reshape(ts, d)
    outs = {k: [] for k in ("pa_k", "pa_v", "pb_k", "pb_v", "pconv", "sa_k", "sa_v", "sb_k", "sb_v", "sconv")}
    for l in range(depth):
        gfin = g_final if l == depth - 1 else None
        qkv, gate, tail = _inproj(xp, g_mix[l], w_in[l], ROW_TILE)
        oa, ob = _attn_prompt(qkv, bias_a_p, bias_b_p, l, _sink_rows(a_sink[l], PAIR))
        xp = _mixout(xp, oa, ob, gate, woa[l], wob[l], wout[l], g_xattn[l], wxq[l], pm_k_bf, pm_v_bf, l,
                     wxo[l], ROW_TILE)
        xp, cp = _ffn(xp.reshape(batch, seq, d), g_ffn[l], zero_state, wup[l], w_conv[l], b_conv[l],
                      wdown[l], gfin, ROW_TILE)
        xp = xp.reshape(seq, d)
        outs["pb_k"].append(tail[ROW_TILE - nbk:, 0:B_W].reshape(batch, nbk, B_HEADS, HEAD_DIM))
        outs["pb_v"].append(tail[ROW_TILE - nbk:, B_W:2 * B_W].reshape(batch, nbk, B_HEADS, HEAD_DIM))
        outs["pa_k"].append(tail[ROW_TILE - na:, 2 * B_W:2 * B_W + A_KV].reshape(batch, na, A_KV_HEADS, HEAD_DIM))
        outs["pa_v"].append(tail[ROW_TILE - na:, 2 * B_W + A_KV:].reshape(batch, na, A_KV_HEADS, HEAD_DIM))
        outs["pconv"].append(cp)
        qkv, gate, tail = _inproj(xs, g_mix[l], w_in[l], ts)
        oa, ob = _attn_sample(qkv, cak, cav, cbk, cbv, l, dec_seq, bias_a_sc, bias_a_sn,
                              bias_b_sc[l], bias_b_sn[l], _sink_rows(a_sink[l], dec_seq))
        xs = _merge(xs, oa, ob, gate, woa[l], wob[l], wout[l], ts)
        xs = _cross(xs.reshape(dec_batch, dec_seq, d), g_xattn[l], wxq[l], cmk, cmv, wxo[l], dec_seq,
                    lambda b, l=l: (l, b, 0, 0))
        xs, cs = _ffn(xs, g_ffn[l], state_conv[l], wup[l], w_conv[l], b_conv[l], wdown[l], gfin, dec_seq)
        xs = xs.reshape(ts, d)
        outs["sb_k"].append(tail[:, 0:B_W].reshape(dec_batch, dec_seq, B_HEADS, HEAD_DIM))
        outs["sb_v"].append(tail[:, B_W:2 * B_W].reshape(dec_batch, dec_seq, B_HEADS, HEAD_DIM))
        outs["sa_k"].append(tail[:, 2 * B_W:2 * B_W + A_KV].reshape(dec_batch, dec_seq, A_KV_HEADS, HEAD_DIM))
        outs["sa_v"].append(tail[:, 2 * B_W + A_KV:].reshape(dec_batch, dec_seq, A_KV_HEADS, HEAD_DIM))
        outs["sconv"].append(cs)

    st = {k: jnp.stack(v) for k, v in outs.items()}
    pm_k = pm_k.reshape(depth, batch, n_mem, X_HEADS, X_HEAD_DIM)
    pm_v = pm_v.reshape(depth, batch, n_mem, X_HEADS, X_HEAD_DIM)
    return (xp.reshape(batch, seq, d), xs.reshape(dec_batch, dec_seq, d),
            st["pa_k"], st["pa_v"], st["pb_k"], st["pb_v"], pm_k, pm_v, st["pconv"],
            st["sa_k"], st["sa_v"], st["sb_k"], st["sb_v"], st["sconv"])
```

```python
import functools

import numpy as np
import jax
import jax.numpy as jnp
from jax import lax
from jax.experimental import pallas as pl
from jax.experimental.pallas import tpu as pltpu

CHUNK = 64
HEAD_DIM = 64
A_HEADS = 8
A_KV_HEADS = 2
A_GROUP = A_HEADS // A_KV_HEADS
A_BACK = 2
B_HEADS = 8
B_BACK = 8
REL_CLIP = 256
X_HEADS = 4
X_HEAD_DIM = 128
EPS = 1e-6
NEG = -1e30
LOG2E = float(np.log2(np.e))

A_Q = A_HEADS * HEAD_DIM
A_KV = A_KV_HEADS * HEAD_DIM
B_W = B_HEADS * HEAD_DIM
X_W = X_HEADS * X_HEAD_DIM
CONV_W = 3
PAST_LEN = 1024

PAIR = 2 * CHUNK
ROW_TILE = 512
A_BAND = (A_BACK + 2) * CHUNK
B_BAND = (B_BACK + 2) * CHUNK
FF_CHUNK = 256
FF_DOWN_SPLIT = 8
QK_AHEAD = 2

V7X_VMEM_LIMIT = 56 * 1024 * 1024

F32 = jnp.float32
BF16 = jnp.bfloat16


def _params(n_axes=1):
    return pltpu.CompilerParams(
        dimension_semantics=("arbitrary",) * n_axes,
        vmem_limit_bytes=V7X_VMEM_LIMIT)


def _resident(shape, index_map):
    return pl.BlockSpec(shape, index_map, pipeline_mode=pl.Buffered(1))


def _rmsnorm(x, g):
    ms = jnp.mean(x * x, axis=-1, keepdims=True)
    return (x * lax.rsqrt(ms + EPS)) * g


def _dot(a, b):
    return jnp.dot(a, b, preferred_element_type=F32)


def _dot_nt(a, b):
    return lax.dot_general(a, b, (((1,), (1,)), ((), ())), preferred_element_type=F32)


def _softmax_pv(s_list, v_list, sink=None):
    m = s_list[0].max(axis=-1, keepdims=True)
    for s in s_list[1:]:
        m = jnp.maximum(m, s.max(axis=-1, keepdims=True))
    if sink is not None:
        m = jnp.maximum(m, sink)
    l = None
    o = None
    for s, v in zip(s_list, v_list):
        p = jnp.exp2(s - m)
        ls = p.sum(axis=-1, keepdims=True)
        os_ = _dot(p.astype(BF16), v)
        l = ls if l is None else l + ls
        o = os_ if o is None else o + os_
    if sink is not None:
        l = l + jnp.exp2(sink - m)
    return o * (1.0 / l)


def _lane_lo(shape):
    return lax.broadcasted_iota(jnp.int32, shape, len(shape) - 1) < HEAD_DIM


def _memkv_kernel(mem_ref, g_ref, wk_ref, wv_ref, k_ref, v_ref):
    hm = _rmsnorm(mem_ref[...], g_ref[...]).astype(BF16)
    k_ref[...] = _dot(hm, wk_ref[...])
    v_ref[...] = _dot(hm, wv_ref[...])


def _memkv(mem, g_mem, w_xk, w_xv):
    depth, d, xw = w_xk.shape
    n_mem = mem.shape[0]
    return pl.pallas_call(
        _memkv_kernel,
        grid=(depth,),
        in_specs=[
            pl.BlockSpec((n_mem, d), lambda l: (0, 0)),
            pl.BlockSpec((None, 1, d), lambda l: (l, 0, 0)),
            pl.BlockSpec((None, d, xw), lambda l: (l, 0, 0)),
            pl.BlockSpec((None, d, xw), lambda l: (l, 0, 0)),
        ],
        out_specs=[
            pl.BlockSpec((None, n_mem, xw), lambda l: (l, 0, 0)),
            pl.BlockSpec((None, n_mem, xw), lambda l: (l, 0, 0)),
        ],
        out_shape=[jax.ShapeDtypeStruct((depth, n_mem, xw), F32)] * 2,
        compiler_params=_params(),
        name="mem_kv",
    )(mem, g_mem.reshape(depth, 1, d), w_xk, w_xv)


N_GATE = 2048
N_QKV = A_Q + 3 * B_W + 2 * A_KV
N_TAIL = 2 * B_W + 2 * A_KV


def _inproj_kernel(x_ref, g_ref, w_ref, qkv_ref, gate_ref, tail_ref):
    h = _rmsnorm(x_ref[...], g_ref[...]).astype(BF16)
    gate_ref[...] = jax.nn.sigmoid(_dot(h, w_ref[:, :N_GATE])).astype(BF16)
    z = _dot(h, w_ref[:, N_GATE:])
    qkv_ref[...] = z.astype(BF16)

    @pl.when(pl.program_id(0) == pl.num_programs(0) - 1)
    def _():
        tail_ref[...] = z[:, N_QKV - N_TAIL:]


def _inproj(x, g, w, tm):
    t, d = x.shape
    n = w.shape[1]
    return pl.pallas_call(
        _inproj_kernel,
        grid=(t // tm,),
        in_specs=[
            pl.BlockSpec((tm, d), lambda i: (i, 0)),
            _resident((1, d), lambda i: (0, 0)),
            _resident((d, n), lambda i: (0, 0)),
        ],
        out_specs=[
            pl.BlockSpec((tm, N_QKV), lambda i: (i, 0)),
            pl.BlockSpec((tm, N_GATE), lambda i: (i, 0)),
            pl.BlockSpec((tm, N_TAIL), lambda i: (0, 0)),
        ],
        out_shape=[
            jax.ShapeDtypeStruct((t, N_QKV), BF16),
            jax.ShapeDtypeStruct((t, N_GATE), BF16),
            jax.ShapeDtypeStruct((tm, N_TAIL), F32),
        ],
        compiler_params=_params(),
        name="inproj",
    )(x, g.reshape(1, d), w)


def _attn_prompt_kernel(qa_ref, qb_ref, kbc_ref, kbp_ref, vbc_ref, vbp_ref,
                        kvac_ref, kvap_ref, bias_a_ref, bias_b_ref, sink_ref,
                        oa_ref, ob_ref, kcat, vcat, kvacat):
    tm = qa_ref.shape[0]
    npair = tm // PAIR
    first_tile = pl.program_id(0) == 0
    kcat[0:tm, :] = kbp_ref[...]
    kcat[tm:2 * tm, :] = kbc_ref[...]
    vcat[0:tm, :] = vbp_ref[...]
    vcat[tm:2 * tm, :] = vbc_ref[...]
    kvacat[0:PAIR, :] = kvap_ref[...]
    kvacat[PAIR:PAIR + tm, :] = kvac_ref[...]
    lo = _lane_lo((PAIR, 2 * HEAD_DIM))
    lo_a = _lane_lo((A_GROUP * PAIR, 2 * HEAD_DIM))

    def pair_body(d, carry):
        r0 = pl.multiple_of(d * PAIR, PAIR)
        var = jnp.where(first_tile, d, npair)
        ka2 = kvacat[pl.ds(r0, A_BAND), 0:A_KV]
        va2 = kvacat[pl.ds(r0, A_BAND), A_KV:2 * A_KV]
        qa = qa_ref[pl.ds(r0, PAIR), :]
        q_stack = jnp.concatenate(
            [qa[:, p * 2 * HEAD_DIM:(p + 1) * 2 * HEAD_DIM] for p in range(A_GROUP)], axis=0)
        b_operands = {}

        def b_group(hp):
            if hp not in b_operands:
                c0 = hp * 2 * HEAD_DIM
                b_operands[hp] = (qb_ref[pl.ds(r0, PAIR), c0:c0 + 2 * HEAD_DIM],
                                  kcat[pl.ds(r0, B_BAND), c0:c0 + 2 * HEAD_DIM],
                                  vcat[pl.ds(r0, B_BAND), c0:c0 + 2 * HEAD_DIM])
            return b_operands[hp]

        def scores(stage):
            if stage[0] == "a":
                kv = stage[1]
                qm = jnp.where(lo_a, q_stack, 0) if kv == 0 else jnp.where(lo_a, 0, q_stack)
                return _dot_nt(qm, ka2) + bias_a_ref[var, kv]
            _, hp, half = stage
            q2, k2, _ = b_group(hp)
            qm = jnp.where(lo, q2, 0) if half == 0 else jnp.where(lo, 0, q2)
            return _dot_nt(qm, k2) + bias_b_ref[var, 2 * hp + half]

        stages = ([("a", kv) for kv in range(A_KV_HEADS)]
                  + [("b", hp, half) for hp in range(B_HEADS // 2) for half in range(2)])
        done = {}
        pending = [scores(st) for st in stages[:QK_AHEAD]]
        for n, stage in enumerate(stages):
            if n + QK_AHEAD < len(stages):
                pending.append(scores(stages[n + QK_AHEAD]))
            s = pending.pop(0)
            if stage[0] == "a":
                done[stage] = _softmax_pv([s], [va2], sink_ref[stage[1]])
                if stage[1] == A_KV_HEADS - 1:
                    for p in range(A_GROUP):
                        rows = slice(p * PAIR, (p + 1) * PAIR)
                        o_pair = jnp.where(lo, done[("a", 0)][rows], done[("a", 1)][rows])
                        oa_ref[pl.ds(r0, PAIR), p * 2 * HEAD_DIM:(p + 1) * 2 * HEAD_DIM] = o_pair.astype(BF16)
            else:
                _, hp, half = stage
                done[stage] = _softmax_pv([s], [b_group(hp)[2]])
                if half == 1:
                    c0 = hp * 2 * HEAD_DIM
                    ob_ref[pl.ds(r0, PAIR), c0:c0 + 2 * HEAD_DIM] = jnp.where(
                        lo, done[("b", hp, 0)], done[("b", hp, 1)]).astype(BF16)
        return carry

    lax.fori_loop(0, tm // PAIR, pair_body, 0)


def _attn_prompt(qkv, bias_a, bias_b, layer, sink_rows):
    t = qkv.shape[0]
    tm = ROW_TILE
    npair = tm // PAIR
    prev = lambda i: jnp.maximum(i - 1, 0)
    return pl.pallas_call(
        _attn_prompt_kernel,
        grid=(t // tm,),
        in_specs=[
            pl.BlockSpec((tm, A_Q), lambda i: (i, 0)),
            pl.BlockSpec((tm, B_W), lambda i: (i, 1)),
            pl.BlockSpec((tm, B_W), lambda i: (i, 2)),
            pl.BlockSpec((tm, B_W), lambda i: (prev(i), 2)),
            pl.BlockSpec((tm, B_W), lambda i: (i, 3)),
            pl.BlockSpec((tm, B_W), lambda i: (prev(i), 3)),
            pl.BlockSpec((tm, 2 * A_KV), lambda i: (i, (A_Q + 3 * B_W) // (2 * A_KV))),
            pl.BlockSpec((PAIR, 2 * A_KV),
                         lambda i: (jnp.maximum(i * npair - 1, 0), (A_Q + 3 * B_W) // (2 * A_KV))),
            _resident(bias_a.shape, lambda i: (0, 0, 0, 0)),
            _resident((None,) + bias_b.shape[1:], lambda i: (layer, 0, 0, 0, 0)),
            _resident(sink_rows.shape, lambda i: (0, 0, 0)),
        ],
        out_specs=[
            pl.BlockSpec((tm, A_Q), lambda i: (i, 0)),
            pl.BlockSpec((tm, B_W), lambda i: (i, 0)),
        ],
        out_shape=[jax.ShapeDtypeStruct((t, A_Q), BF16), jax.ShapeDtypeStruct((t, B_W), BF16)],
        scratch_shapes=[
            pltpu.VMEM((2 * tm, B_W), BF16),
            pltpu.VMEM((2 * tm, B_W), BF16),
            pltpu.VMEM((PAIR + tm, 2 * A_KV), BF16),
        ],
        compiler_params=_params(),
        name="attn_prompt",
    )(qkv, qkv, qkv, qkv, qkv, qkv, qkv, qkv, bias_a, bias_b, sink_rows)


def _attn_sample_kernel(qa_ref, qb_ref, kbn_ref, vbn_ref, kvan_ref,
                        cak_ref, cav_ref, cbk_ref, cbv_ref,
                        bias_ac_ref, bias_an_ref, bias_bc_ref, bias_bn_ref, sink_ref,
                        oa_ref, ob_ref):
    tn = qa_ref.shape[0]
    lo = _lane_lo((tn, 2 * HEAD_DIM))
    lo_a = _lane_lo((A_GROUP * tn, 2 * HEAD_DIM))
    kac = cak_ref[...].astype(BF16)
    vac = cav_ref[...].astype(BF16)
    kan = kvan_ref[:, 0:A_KV]
    van = kvan_ref[:, A_KV:2 * A_KV]
    qa = qa_ref[...]
    q_stack = jnp.concatenate(
        [qa[:, p * 2 * HEAD_DIM:(p + 1) * 2 * HEAD_DIM] for p in range(A_GROUP)], axis=0)
    o_kv = []
    for kv in range(A_KV_HEADS):
        qm = jnp.where(lo_a, q_stack, 0) if kv == 0 else jnp.where(lo_a, 0, q_stack)
        s_c = _dot_nt(qm, kac) + bias_ac_ref[kv]
        s_n = _dot_nt(qm, kan) + bias_an_ref[kv]
        o_kv.append(_softmax_pv([s_c, s_n], [vac, van], sink_ref[kv]))
    for p in range(A_GROUP):
        o_pair = jnp.where(lo, o_kv[0][p * tn:(p + 1) * tn], o_kv[1][p * tn:(p + 1) * tn])
        oa_ref[:, p * 2 * HEAD_DIM:(p + 1) * 2 * HEAD_DIM] = o_pair.astype(BF16)
    for hp in range(B_HEADS // 2):
        c0 = hp * 2 * HEAD_DIM
        q2 = qb_ref[:, c0:c0 + 2 * HEAD_DIM]
        kc = cbk_ref[:, c0:c0 + 2 * HEAD_DIM].astype(BF16)
        vc = cbv_ref[:, c0:c0 + 2 * HEAD_DIM].astype(BF16)
        kn = kbn_ref[:, c0:c0 + 2 * HEAD_DIM]
        vn = vbn_ref[:, c0:c0 + 2 * HEAD_DIM]
        o_h = []
        for half in range(2):
            qm = jnp.where(lo, q2, 0) if half == 0 else jnp.where(lo, 0, q2)
            s_c = _dot_nt(qm, kc) + bias_bc_ref[2 * hp + half]
            s_n = _dot_nt(qm, kn) + bias_bn_ref[2 * hp + half]
            o_h.append(_softmax_pv([s_c, s_n], [vc, vn]))
        ob_ref[:, c0:c0 + 2 * HEAD_DIM] = jnp.where(lo, o_h[0], o_h[1]).astype(BF16)


def _attn_sample(qkv, cak, cav, cbk, cbv, layer, tn, bias_ac, bias_an, bias_bc, bias_bn, sink_rows):
    t = qkv.shape[0]
    nb = t // tn
    la = cak.shape[2]
    lb = cbk.shape[2]
    kva_blk = (A_Q + 3 * B_W) // (2 * A_KV)
    const3 = lambda b: (0, 0, 0)
    return pl.pallas_call(
        _attn_sample_kernel,
        grid=(nb,),
        in_specs=[
            pl.BlockSpec((tn, A_Q), lambda b: (b, 0)),
            pl.BlockSpec((tn, B_W), lambda b: (b, 1)),
            pl.BlockSpec((tn, B_W), lambda b: (b, 2)),
            pl.BlockSpec((tn, B_W), lambda b: (b, 3)),
            pl.BlockSpec((tn, 2 * A_KV), lambda b: (b, kva_blk)),
            pl.BlockSpec((None, None, la, A_KV), lambda b: (layer, b, 0, 0)),
            pl.BlockSpec((None, None, la, A_KV), lambda b: (layer, b, 0, 0)),
            pl.BlockSpec((None, None, lb, B_W), lambda b: (layer, b, 0, 0)),
            pl.BlockSpec((None, None, lb, B_W), lambda b: (layer, b, 0, 0)),
            _resident(bias_ac.shape, const3),
            _resident(bias_an.shape, const3),
            _resident(bias_bc.shape, const3),
            _resident(bias_bn.shape, const3),
            _resident(sink_rows.shape, const3),
        ],
        out_specs=[
            pl.BlockSpec((tn, A_Q), lambda b: (b, 0)),
            pl.BlockSpec((tn, B_W), lambda b: (b, 0)),
        ],
        out_shape=[jax.ShapeDtypeStruct((t, A_Q), BF16), jax.ShapeDtypeStruct((t, B_W), BF16)],
        compiler_params=_params(),
        name="attn_sample",
    )(qkv, qkv, qkv, qkv, qkv, cak, cav, cbk, cbv, bias_ac, bias_an, bias_bc, bias_bn, sink_rows)


def _merge_rows(x, oa_ref, ob_ref, gate_ref, woa_ref, wob_ref, wout_ref):
    d = x.shape[1]
    a = _dot(oa_ref[...], woa_ref[...])
    b = _dot(ob_ref[...], wob_ref[...])
    m = gate_ref[:, 0:d].astype(F32) * a + gate_ref[:, d:2 * d].astype(F32) * b
    return x + _dot(m.astype(BF16), wout_ref[...])


def _merge_kernel(x_ref, oa_ref, ob_ref, gate_ref, woa_ref, wob_ref, wout_ref, o_ref):
    o_ref[...] = _merge_rows(x_ref[...], oa_ref, ob_ref, gate_ref, woa_ref, wob_ref, wout_ref)


def _merge(x, oa, ob, gate, woa, wob, wout, tm):
    t, d = x.shape
    row = lambda i: (i, 0)
    const = lambda i: (0, 0)
    return pl.pallas_call(
        _merge_kernel,
        grid=(t // tm,),
        in_specs=[
            pl.BlockSpec((tm, d), row),
            pl.BlockSpec((tm, A_Q), row),
            pl.BlockSpec((tm, B_W), row),
            pl.BlockSpec((tm, 2 * d), row),
            _resident(woa.shape, const),
            _resident(wob.shape, const),
            _resident(wout.shape, const),
        ],
        out_specs=pl.BlockSpec((tm, d), row),
        out_shape=jax.ShapeDtypeStruct((t, d), F32),
        compiler_params=_params(),
        name="merge",
    )(x, oa, ob, gate, woa, wob, wout)


def _cross_rows(x, g_ref, wq_ref, mk_ref, mv_ref, wo_ref):
    h = _rmsnorm(x, g_ref[...]).astype(BF16)
    q = _dot(h, wq_ref[...]).astype(BF16)
    mk = mk_ref[...].astype(BF16)
    mv = mv_ref[...].astype(BF16)
    scale = X_HEAD_DIM ** -0.5 * LOG2E

    def scores(hd):
        c0 = hd * X_HEAD_DIM
        return _dot_nt(q[:, c0:c0 + X_HEAD_DIM], mk[:, c0:c0 + X_HEAD_DIM]) * scale

    pending = [scores(hd) for hd in range(min(QK_AHEAD, X_HEADS))]
    outs = []
    for hd in range(X_HEADS):
        if hd + QK_AHEAD < X_HEADS:
            pending.append(scores(hd + QK_AHEAD))
        c0 = hd * X_HEAD_DIM
        outs.append(_softmax_pv([pending.pop(0)], [mv[:, c0:c0 + X_HEAD_DIM]]))
    o = jnp.concatenate(outs, axis=-1).astype(BF16)
    return x + _dot(o, wo_ref[...])


def _cross_kernel(x_ref, g_ref, wq_ref, mk_ref, mv_ref, wo_ref, o_ref):
    o_ref[...] = _cross_rows(x_ref[...], g_ref, wq_ref, mk_ref, mv_ref, wo_ref)


def _mixout_kernel(x_ref, oa_ref, ob_ref, gate_ref, woa_ref, wob_ref, wout_ref,
                   g_ref, wq_ref, mk_ref, mv_ref, wo_ref, o_ref):
    x = _merge_rows(x_ref[...], oa_ref, ob_ref, gate_ref, woa_ref, wob_ref, wout_ref)
    o_ref[...] = _cross_rows(x, g_ref, wq_ref, mk_ref, mv_ref, wo_ref)


def _mixout(x, oa, ob, gate, woa, wob, wout, g, wq, mk, mv, layer, wo, tm):
    t, d = x.shape
    n_mem = mk.shape[-2]
    row = lambda i: (i, 0)
    const = lambda i: (0, 0)
    return pl.pallas_call(
        _mixout_kernel,
        grid=(t // tm,),
        in_specs=[
            pl.BlockSpec((tm, d), row),
            pl.BlockSpec((tm, A_Q), row),
            pl.BlockSpec((tm, B_W), row),
            pl.BlockSpec((tm, 2 * d), row),
            _resident(woa.shape, const),
            _resident(wob.shape, const),
            _resident(wout.shape, const),
            _resident((1, d), const),
            _resident(wq.shape, const),
            _resident((None, n_mem, X_W), lambda i: (layer, 0, 0)),
            _resident((None, n_mem, X_W), lambda i: (layer, 0, 0)),
            _resident(wo.shape, const),
        ],
        out_specs=pl.BlockSpec((tm, d), row),
        out_shape=jax.ShapeDtypeStruct((t, d), F32),
        compiler_params=_params(),
        name="mix_out",
    )(x, oa, ob, gate, woa, wob, wout, g.reshape(1, d), wq, mk, mv, wo)


def _cross(x3, g, wq, mk, mv, wo, tm, kv_index):
    nb, rows, d = x3.shape
    n_mem = mk.shape[-2]
    kv_block = (None,) * (mk.ndim - 2) + (n_mem, X_W)
    const = lambda b, i: (0, 0)
    return pl.pallas_call(
        _cross_kernel,
        grid=(nb, rows // tm),
        in_specs=[
            pl.BlockSpec((None, tm, d), lambda b, i: (b, i, 0)),
            _resident((1, d), const),
            _resident(wq.shape, const),
            pl.BlockSpec(kv_block, lambda b, i: kv_index(b)),
            pl.BlockSpec(kv_block, lambda b, i: kv_index(b)),
            _resident(wo.shape, const),
        ],
        out_specs=pl.BlockSpec((None, tm, d), lambda b, i: (b, i, 0)),
        out_shape=jax.ShapeDtypeStruct((nb, rows, d), F32),
        compiler_params=_params(2),
        name="cross",
    )(x3, g.reshape(1, d), wq, mk, mv, wo)


def _ffn_kernel(*refs, nb, tm, d_ff, final):
    if final:
        (x_ref, g_ref, state_ref, wup_ref, wconv_ref, bconv_ref, wdown_ref, gfin_ref,
         o_ref, ctail_ref, carry, ebuf, act) = refs
    else:
        (x_ref, g_ref, state_ref, wup_ref, wconv_ref, bconv_ref, wdown_ref,
         o_ref, ctail_ref, carry, ebuf, act) = refs
    i = pl.program_id(0)
    d = x_ref.shape[-1]
    hist = CONV_W - 1
    top = 8

    @pl.when(i == 0)
    def _():
        carry[:, top - hist:top, :] = state_ref[...]

    x = x_ref[...].reshape(nb * tm, d)
    h = _rmsnorm(x, g_ref[...]).astype(BF16)
    n_chunks = d_ff // FF_CHUNK
    for j in range(n_chunks):
        c_half = []
        for half in range(2):
            c0 = half * d_ff + j * FF_CHUNK
            slot = half * n_chunks + j
            u = _dot(h, wup_ref[:, c0:c0 + FF_CHUNK]).reshape(nb, tm, FF_CHUNK)
            ebuf[slot, :, top - hist:top, :] = carry[:, top - hist:top, c0:c0 + FF_CHUNK]
            ebuf[slot, :, top:top + tm, :] = u
            carry[:, top - hist:top, c0:c0 + FF_CHUNK] = u[:, tm - hist:tm, :]
            c = bconv_ref[:, c0:c0 + FF_CHUNK]
            for k in range(hist):
                c = c + ebuf[slot, :, top - hist + k:top - hist + k + tm, :] * wconv_ref[k:k + 1, c0:c0 + FF_CHUNK]
            c = c + u * wconv_ref[hist:hist + 1, c0:c0 + FF_CHUNK]
            c_half.append(c)
        a = jax.nn.silu(c_half[0]) * c_half[1]
        act[:, j * FF_CHUNK:(j + 1) * FF_CHUNK] = a.reshape(nb * tm, FF_CHUNK).astype(BF16)
    k0 = FF_DOWN_SPLIT * FF_CHUNK
    out = x + (_dot(act[:, :k0], wdown_ref[:k0, :]) + _dot(act[:, k0:], wdown_ref[k0:, :]))
    if final:
        out = _rmsnorm(out, gfin_ref[...])
    o_ref[...] = out.reshape(nb, tm, d)

    @pl.when(i == pl.num_programs(0) - 1)
    def _():
        ctail_ref[...] = carry[:, top - hist:top, :]


def _ffn(x3, g, state, wup, wconv, bconv, wdown, gfin, tm):
    nb, rows, d = x3.shape
    d_ff = wdown.shape[0]
    final = gfin is not None
    const = lambda i: (0, 0)
    in_specs = [
        pl.BlockSpec((nb, tm, d), lambda i: (0, i, 0)),
        _resident((1, d), const),
        _resident(state.shape, lambda i: (0, 0, 0)),
        _resident(wup.shape, const),
        _resident(wconv.shape, const),
        _resident((1, 2 * d_ff), const),
        _resident(wdown.shape, const),
    ]
    args = [x3, g.reshape(1, d), state, wup, wconv, bconv.reshape(1, 2 * d_ff), wdown]
    if final:
        in_specs.append(_resident((1, d), const))
        args.append(gfin.reshape(1, d))
    return pl.pallas_call(
        functools.partial(_ffn_kernel, nb=nb, tm=tm, d_ff=d_ff, final=final),
        grid=(rows // tm,),
        in_specs=in_specs,
        out_specs=[
            pl.BlockSpec((nb, tm, d), lambda i: (0, i, 0)),
            pl.BlockSpec((nb, CONV_W - 1, 2 * d_ff), lambda i: (0, 0, 0)),
        ],
        out_shape=[
            jax.ShapeDtypeStruct((nb, rows, d), F32),
            jax.ShapeDtypeStruct((nb, CONV_W - 1, 2 * d_ff), F32),
        ],
        scratch_shapes=[
            pltpu.VMEM((nb, 8, 2 * d_ff), F32),
            pltpu.VMEM((2 * (d_ff // FF_CHUNK), nb, 8 + tm, FF_CHUNK), F32),
            pltpu.VMEM((nb * tm, d_ff), BF16),
        ],
        compiler_params=_params(),
        name="conv_ffn",
    )(*args)


def _alibi_slopes():
    return np.exp2(-8.0 * np.arange(1, A_HEADS + 1) / A_HEADS).astype(np.float32)


def _pair_major(w, axis):
    shp = w.shape
    w = w.reshape(shp[:axis] + (A_KV_HEADS, A_GROUP, HEAD_DIM) + shp[axis + 1:])
    w = jnp.swapaxes(w, axis, axis + 1)
    return w.reshape(shp)


def _prep_w_in(w):
    d = w.shape[1]
    c = np.cumsum([0, A_Q, A_KV, A_KV, B_W, B_W, B_W, d, d])
    qa, ka, va, qb, kb, vb, ga, gb = [w[..., c[i]:c[i + 1]] for i in range(8)]
    qscale = HEAD_DIM ** -0.5 * LOG2E
    qa = _pair_major(qa, 2) * qscale
    return jnp.concatenate([ga, gb, qa, qb * qscale, kb, vb, ka, va], axis=-1).astype(BF16)


def _band_valid(back, band, tm):
    npair = tm // PAIR
    r = np.arange(PAIR)[:, None]
    c = np.arange(band)[None, :]
    qc, kc = r // CHUNK, c // CHUNK
    in_band = (kc >= qc) & (kc <= qc + back)
    history = band - PAIR
    variants = [in_band & (j * PAIR + c >= history) for j in range(npair)] + [in_band]
    return np.where(np.stack(variants), 0.0, NEG).astype(np.float32)


def _prompt_bias_a(tm):
    r = np.arange(PAIR)[:, None]
    c = np.arange(A_BAND)[None, :]
    dist = np.abs(A_BACK * CHUNK + r - c).astype(np.float32)
    slopes = _alibi_slopes().reshape(1, A_KV_HEADS, A_GROUP, 1, 1)
    bias = (-slopes * dist + _band_valid(A_BACK, A_BAND, tm)[:, None, None]) * np.float32(LOG2E)
    return jnp.asarray(bias.reshape(-1, A_KV_HEADS, A_GROUP * PAIR, A_BAND))


def _take_clipped(table, idx):
    idx = np.asarray(idx)
    n_lo = int(np.sum(idx == idx[0])) - 1
    n_hi = int(np.sum(idx == idx[-1])) - 1 if idx[-1] != idx[0] else 0
    want = np.concatenate([np.full(n_lo, idx[0]), np.arange(idx[0], idx[-1] + 1), np.full(n_hi, idx[-1])])
    assert np.array_equal(want, idx)
    lead = table.shape[:-1]
    parts = [jnp.broadcast_to(table[..., idx[0]:idx[0] + 1], lead + (n_lo,)),
             table[..., idx[0]:idx[-1] + 1],
             jnp.broadcast_to(table[..., idx[-1]:idx[-1] + 1], lead + (n_hi,))]
    return jnp.concatenate(parts, axis=-1)


def _rel_diagonals(table, n_q, offset, width):
    idx = np.clip(np.arange(width) - (n_q - 1) + offset, -REL_CLIP, REL_CLIP) + REL_CLIP
    return _take_clipped(table, idx)


def _rel_bias_kernel(ext_ref, valid_ref, out_ref):
    w = ext_ref.shape[-1]
    for h in range(B_HEADS):
        x = jnp.broadcast_to(ext_ref[h:h + 1, :], (PAIR, w))
        y = pltpu.roll(x, w - (PAIR - 1), 1, stride=1, stride_axis=0)[:, :B_BAND]
        for j in range(valid_ref.shape[0]):
            out_ref[j, h] = y + valid_ref[j]


def _prompt_bias_b(tables, tm):
    depth = tables.shape[0]
    w = 768
    assert w % 128 == 0 and w >= PAIR + B_BAND - 1
    ext = _rel_diagonals(tables, PAIR, -B_BACK * CHUNK, w)
    valid = jnp.asarray(_band_valid(B_BACK, B_BAND, tm) * np.float32(LOG2E))
    nvar = valid.shape[0]
    return pl.pallas_call(
        _rel_bias_kernel,
        grid=(depth,),
        in_specs=[
            pl.BlockSpec((None, B_HEADS, w), lambda l: (l, 0, 0)),
            pl.BlockSpec(valid.shape, lambda l: (0, 0, 0)),
        ],
        out_specs=pl.BlockSpec((None, nvar, B_HEADS, PAIR, B_BAND), lambda l: (l, 0, 0, 0, 0)),
        out_shape=jax.ShapeDtypeStruct((depth, nvar, B_HEADS, PAIR, B_BAND), F32),
        compiler_params=_params(),
        name="rel_bias",
    )(ext, valid)


def _chunk_valid(pos_q, pos_k, back):
    cq, ck = pos_q[:, None] // CHUNK, pos_k[None, :] // CHUNK
    return np.where((ck <= cq) & (ck >= cq - back), 0.0, NEG).astype(np.float32)


def _sample_bias_a(past_len, la, tn):
    pos_q = past_len + np.arange(tn)
    slopes = _alibi_slopes().reshape(A_KV_HEADS, A_GROUP, 1, 1)

    def bias(pos_k):
        dist = np.abs(pos_q[:, None] - pos_k[None, :]).astype(np.float32)
        b = (-slopes * dist + _chunk_valid(pos_q, pos_k, A_BACK)) * np.float32(LOG2E)
        return jnp.asarray(b.reshape(A_KV_HEADS, A_GROUP * tn, pos_k.shape[0]))

    return bias(past_len - la + np.arange(la)), bias(pos_q)


def _sample_bias_b(tables, past_len, lb, tn):
    pos_q = past_len + np.arange(tn)

    def bias(pos_k, offset):
        n_k = pos_k.shape[0]
        ext = _rel_diagonals(tables, tn, offset, n_k + tn - 1)
        rows = [ext[..., tn - 1 - i:tn - 1 - i + n_k] for i in range(tn)]
        valid = _chunk_valid(pos_q, pos_k, B_BACK) * np.float32(LOG2E)
        return jnp.stack(rows, axis=-2) + valid

    return bias(past_len - lb + np.arange(lb), -lb), bias(pos_q, 0)


def _sink_rows(sink, rows):
    s = jnp.repeat(sink.reshape(A_KV_HEADS, A_GROUP, 1) * LOG2E, rows, axis=2)
    return s.reshape(A_KV_HEADS, A_GROUP * rows, 1)


def kernel(x_prompt, x_sample, mem_prompt, cache_a_k, cache_a_v, cache_b_k, cache_b_v, cache_mem_k, cache_mem_v, state_conv, g_mix, w_mix_in, a_sink, b_rel_bias, w_o_a, w_o_b, w_mix_out, g_xattn, g_mem, w_xq, w_xk, w_xv, w_xo, g_ffn, w_up, w_conv, b_conv, w_down, g_final):
    batch, seq, d = x_prompt.shape
    dec_batch, dec_seq, _ = x_sample.shape
    depth = w_mix_in.shape[0]
    n_mem = mem_prompt.shape[1]
    d_ff = w_down.shape[1]
    la, lb = cache_a_k.shape[2], cache_b_k.shape[2]
    past_len = PAST_LEN
    assert batch == 1 and seq % ROW_TILE == 0 and d_ff % FF_CHUNK == 0
    na, nbk = min(A_BACK * CHUNK, seq), min(B_BACK * CHUNK, seq)
    assert nbk == ROW_TILE and na <= ROW_TILE
    ts = dec_batch * dec_seq

    w_in = _prep_w_in(w_mix_in)
    woa = _pair_major(w_o_a, 1).astype(BF16)
    wob = w_o_b.astype(BF16)
    wout = w_mix_out.astype(BF16)
    wxq, wxk, wxv, wxo = (w.astype(BF16) for w in (w_xq, w_xk, w_xv, w_xo))
    wup = w_up.astype(BF16)
    wdown = w_down.astype(BF16)

    rel_tables = b_rel_bias.astype(F32) * LOG2E
    bias_a_p = _prompt_bias_a(ROW_TILE)
    bias_b_p = _prompt_bias_b(rel_tables, ROW_TILE)
    bias_a_sc, bias_a_sn = _sample_bias_a(past_len, la, dec_seq)
    bias_b_sc, bias_b_sn = _sample_bias_b(rel_tables, past_len, lb, dec_seq)
    cak = cache_a_k.reshape(depth, dec_batch, la, A_KV)
    cav = cache_a_v.reshape(depth, dec_batch, la, A_KV)
    cbk = cache_b_k.reshape(depth, dec_batch, lb, B_W)
    cbv = cache_b_v.reshape(depth, dec_batch, lb, B_W)
    cmk = cache_mem_k.reshape(depth, dec_batch, n_mem, X_W)
    cmv = cache_mem_v.reshape(depth, dec_batch, n_mem, X_W)
    zero_state = jnp.zeros((batch, CONV_W - 1, 2 * d_ff), F32)

    pm_k, pm_v = _memkv(mem_prompt[0], g_mem, wxk, wxv)
    pm_k_bf, pm_v_bf = pm_k.astype(BF16), pm_v.astype(BF16)

    xp = x_prompt.reshape(seq, d)
    xs = x_sample.reshape(ts, d)
    outs = {k: [] for k in ("pa_k", "pa_v", "pb_k", "pb_v", "pconv", "sa_k", "sa_v", "sb_k", "sb_v", "sconv")}
    for l in range(depth):
        gfin = g_final if l == depth - 1 else None
        qkv, gate, tail = _inproj(xp, g_mix[l], w_in[l], ROW_TILE)
        oa, ob = _attn_prompt(qkv, bias_a_p, bias_b_p, l, _sink_rows(a_sink[l], PAIR))
        xp = _mixout(xp, oa, ob, gate, woa[l], wob[l], wout[l], g_xattn[l], wxq[l], pm_k_bf, pm_v_bf, l,
                     wxo[l], ROW_TILE)
        xp, cp = _ffn(xp.reshape(batch, seq, d), g_ffn[l], zero_state, wup[l], w_conv[l], b_conv[l],
                      wdown[l], gfin, ROW_TILE)
        xp = xp.reshape(seq, d)
        outs["pb_k"].append(tail[ROW_TILE - nbk:, 0:B_W].reshape(batch, nbk, B_HEADS, HEAD_DIM))
        outs["pb_v"].append(tail[ROW_TILE - nbk:, B_W:2 * B_W].reshape(batch, nbk, B_HEADS, HEAD_DIM))
        outs["pa_k"].append(tail[ROW_TILE - na:, 2 * B_W:2 * B_W + A_KV].reshape(batch, na, A_KV_HEADS, HEAD_DIM))
        outs["pa_v"].append(tail[ROW_TILE - na:, 2 * B_W + A_KV:].reshape(batch, na, A_KV_HEADS, HEAD_DIM))
        outs["pconv"].append(cp)
        qkv, gate, tail = _inproj(xs, g_mix[l], w_in[l], ts)
        oa, ob = _attn_sample(qkv, cak, cav, cbk, cbv, l, dec_seq, bias_a_sc, bias_a_sn,
                              bias_b_sc[l], bias_b_sn[l], _sink_rows(a_sink[l], dec_seq))
        xs = _merge(xs, oa, ob, gate, woa[l], wob[l], wout[l], ts)
        xs = _cross(xs.reshape(dec_batch, dec_seq, d), g_xattn[l], wxq[l], cmk, cmv, wxo[l], dec_seq,
                    lambda b, l=l: (l, b, 0, 0))
        xs, cs = _ffn(xs, g_ffn[l], state_conv[l], wup[l], w_conv[l], b_conv[l], wdown[l], gfin, dec_seq)
        xs = xs.reshape(ts, d)
        outs["sb_k"].append(tail[:, 0:B_W].reshape(dec_batch, dec_seq, B_HEADS, HEAD_DIM))
        outs["sb_v"].append(tail[:, B_W:2 * B_W].reshape(dec_batch, dec_seq, B_HEADS, HEAD_DIM))
        outs["sa_k"].append(tail[:, 2 * B_W:2 * B_W + A_KV].reshape(dec_batch, dec_seq, A_KV_HEADS, HEAD_DIM))
        outs["sa_v"].append(tail[:, 2 * B_W + A_KV:].reshape(dec_batch, dec_seq, A_KV_HEADS, HEAD_DIM))
        outs["sconv"].append(cs)

    st = {k: jnp.stack(v) for k, v in outs.items()}
    pm_k = pm_k.reshape(depth, batch, n_mem, X_HEADS, X_HEAD_DIM)
    pm_v = pm_v.reshape(depth, batch, n_mem, X_HEADS, X_HEAD_DIM)
    return (xp.reshape(batch, seq, d), xs.reshape(dec_batch, dec_seq, d),
            st["pa_k"], st["pa_v"], st["pb_k"], st["pb_v"], pm_k, pm_v, st["pconv"],
            st["sa_k"], st["sa_v"], st["sb_k"], st["sb_v"], st["sconv"])
```

```python
import functools

import numpy as np
import jax
import jax.numpy as jnp
from jax import lax
from jax.experimental import pallas as pl
from jax.experimental.pallas import tpu as pltpu

CHUNK = 64
HEAD_DIM = 64
A_HEADS = 8
A_KV_HEADS = 2
A_GROUP = A_HEADS // A_KV_HEADS
A_BACK = 2
B_HEADS = 8
B_BACK = 8
REL_CLIP = 256
X_HEADS = 4
X_HEAD_DIM = 128
EPS = 1e-6
NEG = -1e30
LOG2E = float(np.log2(np.e))

A_Q = A_HEADS * HEAD_DIM
A_KV = A_KV_HEADS * HEAD_DIM
B_W = B_HEADS * HEAD_DIM
X_W = X_HEADS * X_HEAD_DIM
CONV_W = 3
PAST_LEN = 1024

PAIR = 2 * CHUNK
ROW_TILE = 512
WIDE_TILE = 1024
A_BAND = (A_BACK + 2) * CHUNK
B_BAND = (B_BACK + 2) * CHUNK
FF_CHUNK = 256
FF_DOWN_CUTS = (8,)
QK_AHEAD = 2

V7X_VMEM_LIMIT = 56 * 1024 * 1024

F32 = jnp.float32
BF16 = jnp.bfloat16


def _params(n_axes=1):
    return pltpu.CompilerParams(
        dimension_semantics=("arbitrary",) * n_axes,
        vmem_limit_bytes=V7X_VMEM_LIMIT)


def _resident(shape, index_map):
    return pl.BlockSpec(shape, index_map, pipeline_mode=pl.Buffered(1))


def _rmsnorm(x, g):
    ms = jnp.mean(x * x, axis=-1, keepdims=True)
    return (x * lax.rsqrt(ms + EPS)) * g


def _dot(a, b):
    return jnp.dot(a, b, preferred_element_type=F32)


def _dot_nt(a, b):
    return lax.dot_general(a, b, (((1,), (1,)), ((), ())), preferred_element_type=F32)


def _softmax_pv(s_list, v_list, sink=None):
    m = s_list[0].max(axis=-1, keepdims=True)
    for s in s_list[1:]:
        m = jnp.maximum(m, s.max(axis=-1, keepdims=True))
    if sink is not None:
        m = jnp.maximum(m, sink)
    l = None
    o = None
    for s, v in zip(s_list, v_list):
        p = jnp.exp2(s - m)
        ls = p.sum(axis=-1, keepdims=True)
        os_ = _dot(p.astype(BF16), v)
        l = ls if l is None else l + ls
        o = os_ if o is None else o + os_
    if sink is not None:
        l = l + jnp.exp2(sink - m)
    return o * (1.0 / l)


def _lane_lo(shape):
    return lax.broadcasted_iota(jnp.int32, shape, len(shape) - 1) < HEAD_DIM


def _memkv_kernel(mem_ref, g_ref, wk_ref, wv_ref, k_ref, v_ref):
    hm = _rmsnorm(mem_ref[...], g_ref[...]).astype(BF16)
    k_ref[...] = _dot(hm, wk_ref[...])
    v_ref[...] = _dot(hm, wv_ref[...])


def _memkv(mem, g_mem, w_xk, w_xv):
    depth, d, xw = w_xk.shape
    n_mem = mem.shape[0]
    return pl.pallas_call(
        _memkv_kernel,
        grid=(depth,),
        in_specs=[
            pl.BlockSpec((n_mem, d), lambda l: (0, 0)),
            pl.BlockSpec((None, 1, d), lambda l: (l, 0, 0)),
            pl.BlockSpec((None, d, xw), lambda l: (l, 0, 0)),
            pl.BlockSpec((None, d, xw), lambda l: (l, 0, 0)),
        ],
        out_specs=[
            pl.BlockSpec((None, n_mem, xw), lambda l: (l, 0, 0)),
            pl.BlockSpec((None, n_mem, xw), lambda l: (l, 0, 0)),
        ],
        out_shape=[jax.ShapeDtypeStruct((depth, n_mem, xw), F32)] * 2,
        compiler_params=_params(),
        name="mem_kv",
    )(mem, g_mem.reshape(depth, 1, d), w_xk, w_xv)


N_GATE = 2048
N_QKV = A_Q + 3 * B_W + 2 * A_KV
N_TAIL = 2 * B_W + 2 * A_KV


def _inproj_kernel(x_ref, g_ref, w_ref, qkv_ref, gate_ref, tail_ref):
    h = _rmsnorm(x_ref[...], g_ref[...]).astype(BF16)
    gate_ref[...] = jax.nn.sigmoid(_dot(h, w_ref[:, :N_GATE])).astype(BF16)
    z = _dot(h, w_ref[:, N_GATE:])
    qkv_ref[...] = z.astype(BF16)

    @pl.when(pl.program_id(0) == pl.num_programs(0) - 1)
    def _():
        tail_ref[...] = z[z.shape[0] - tail_ref.shape[0]:, N_QKV - N_TAIL:]


def _inproj(x, g, w, tm, tail_rows):
    t, d = x.shape
    n = w.shape[1]
    assert tail_rows <= tm
    return pl.pallas_call(
        _inproj_kernel,
        grid=(t // tm,),
        in_specs=[
            pl.BlockSpec((tm, d), lambda i: (i, 0)),
            _resident((1, d), lambda i: (0, 0)),
            _resident((d, n), lambda i: (0, 0)),
        ],
        out_specs=[
            pl.BlockSpec((tm, N_QKV), lambda i: (i, 0)),
            pl.BlockSpec((tm, N_GATE), lambda i: (i, 0)),
            pl.BlockSpec((tail_rows, N_TAIL), lambda i: (0, 0)),
        ],
        out_shape=[
            jax.ShapeDtypeStruct((t, N_QKV), BF16),
            jax.ShapeDtypeStruct((t, N_GATE), BF16),
            jax.ShapeDtypeStruct((tail_rows, N_TAIL), F32),
        ],
        compiler_params=_params(),
        name="inproj",
    )(x, g.reshape(1, d), w)


def _attn_prompt_kernel(qa_ref, qb_ref, kbc_ref, kbp_ref, vbc_ref, vbp_ref,
                        kvac_ref, kvap_ref, bias_a_ref, bias_b_ref, sink_ref,
                        oa_ref, ob_ref, kcat, vcat, kvacat):
    tm = qa_ref.shape[0]
    npair = tm // PAIR
    first_tile = pl.program_id(0) == 0
    kcat[0:tm, :] = kbp_ref[...]
    kcat[tm:2 * tm, :] = kbc_ref[...]
    vcat[0:tm, :] = vbp_ref[...]
    vcat[tm:2 * tm, :] = vbc_ref[...]
    kvacat[0:PAIR, :] = kvap_ref[...]
    kvacat[PAIR:PAIR + tm, :] = kvac_ref[...]
    lo = _lane_lo((PAIR, 2 * HEAD_DIM))
    lo_a = _lane_lo((A_GROUP * PAIR, 2 * HEAD_DIM))

    def pair_body(d, carry):
        r0 = pl.multiple_of(d * PAIR, PAIR)
        var = jnp.where(first_tile, d, npair)
        ka2 = kvacat[pl.ds(r0, A_BAND), 0:A_KV]
        va2 = kvacat[pl.ds(r0, A_BAND), A_KV:2 * A_KV]
        qa = qa_ref[pl.ds(r0, PAIR), :]
        q_stack = jnp.concatenate(
            [qa[:, p * 2 * HEAD_DIM:(p + 1) * 2 * HEAD_DIM] for p in range(A_GROUP)], axis=0)
        b_operands = {}

        def b_group(hp):
            if hp not in b_operands:
                c0 = hp * 2 * HEAD_DIM
                b_operands[hp] = (qb_ref[pl.ds(r0, PAIR), c0:c0 + 2 * HEAD_DIM],
                                  kcat[pl.ds(r0, B_BAND), c0:c0 + 2 * HEAD_DIM],
                                  vcat[pl.ds(r0, B_BAND), c0:c0 + 2 * HEAD_DIM])
            return b_operands[hp]

        def scores(stage):
            if stage[0] == "a":
                kv = stage[1]
                qm = jnp.where(lo_a, q_stack, 0) if kv == 0 else jnp.where(lo_a, 0, q_stack)
                return _dot_nt(qm, ka2) + bias_a_ref[var, kv]
            _, hp, half = stage
            q2, k2, _ = b_group(hp)
            qm = jnp.where(lo, q2, 0) if half == 0 else jnp.where(lo, 0, q2)
            return _dot_nt(qm, k2) + bias_b_ref[var, 2 * hp + half]

        stages = ([("a", kv) for kv in range(A_KV_HEADS)]
                  + [("b", hp, half) for hp in range(B_HEADS // 2) for half in range(2)])
        done = {}
        pending = [scores(st) for st in stages[:QK_AHEAD]]
        for n, stage in enumerate(stages):
            if n + QK_AHEAD < len(stages):
                pending.append(scores(stages[n + QK_AHEAD]))
            s = pending.pop(0)
            if stage[0] == "a":
                done[stage] = _softmax_pv([s], [va2], sink_ref[stage[1]])
                if stage[1] == A_KV_HEADS - 1:
                    for p in range(A_GROUP):
                        rows = slice(p * PAIR, (p + 1) * PAIR)
                        o_pair = jnp.where(lo, done[("a", 0)][rows], done[("a", 1)][rows])
                        oa_ref[pl.ds(r0, PAIR), p * 2 * HEAD_DIM:(p + 1) * 2 * HEAD_DIM] = o_pair.astype(BF16)
            else:
                _, hp, half = stage
                done[stage] = _softmax_pv([s], [b_group(hp)[2]])
                if half == 1:
                    c0 = hp * 2 * HEAD_DIM
                    ob_ref[pl.ds(r0, PAIR), c0:c0 + 2 * HEAD_DIM] = jnp.where(
                        lo, done[("b", hp, 0)], done[("b", hp, 1)]).astype(BF16)
        return carry

    lax.fori_loop(0, tm // PAIR, pair_body, 0)


def _attn_prompt(qkv, bias_a, bias_b, layer, sink_rows):
    t = qkv.shape[0]
    tm = ROW_TILE
    npair = tm // PAIR
    prev = lambda i: jnp.maximum(i - 1, 0)
    return pl.pallas_call(
        _attn_prompt_kernel,
        grid=(t // tm,),
        in_specs=[
            pl.BlockSpec((tm, A_Q), lambda i: (i, 0)),
            pl.BlockSpec((tm, B_W), lambda i: (i, 1)),
            pl.BlockSpec((tm, B_W), lambda i: (i, 2)),
            pl.BlockSpec((tm, B_W), lambda i: (prev(i), 2)),
            pl.BlockSpec((tm, B_W), lambda i: (i, 3)),
            pl.BlockSpec((tm, B_W), lambda i: (prev(i), 3)),
            pl.BlockSpec((tm, 2 * A_KV), lambda i: (i, (A_Q + 3 * B_W) // (2 * A_KV))),
            pl.BlockSpec((PAIR, 2 * A_KV),
                         lambda i: (jnp.maximum(i * npair - 1, 0), (A_Q + 3 * B_W) // (2 * A_KV))),
            _resident(bias_a.shape, lambda i: (0, 0, 0, 0)),
            _resident((None,) + bias_b.shape[1:], lambda i: (layer, 0, 0, 0, 0)),
            _resident(sink_rows.shape, lambda i: (0, 0, 0)),
        ],
        out_specs=[
            pl.BlockSpec((tm, A_Q), lambda i: (i, 0)),
            pl.BlockSpec((tm, B_W), lambda i: (i, 0)),
        ],
        out_shape=[jax.ShapeDtypeStruct((t, A_Q), BF16), jax.ShapeDtypeStruct((t, B_W), BF16)],
        scratch_shapes=[
            pltpu.VMEM((2 * tm, B_W), BF16),
            pltpu.VMEM((2 * tm, B_W), BF16),
            pltpu.VMEM((PAIR + tm, 2 * A_KV), BF16),
        ],
        compiler_params=_params(),
        name="attn_prompt",
    )(qkv, qkv, qkv, qkv, qkv, qkv, qkv, qkv, bias_a, bias_b, sink_rows)


def _attn_sample_kernel(qa_ref, qb_ref, kbn_ref, vbn_ref, kvan_ref,
                        cak_ref, cav_ref, cbk_ref, cbv_ref,
                        bias_ac_ref, bias_an_ref, bias_bc_ref, bias_bn_ref, sink_ref,
                        oa_ref, ob_ref):
    tn = qa_ref.shape[0]
    lo = _lane_lo((tn, 2 * HEAD_DIM))
    lo_a = _lane_lo((A_GROUP * tn, 2 * HEAD_DIM))
    kac = cak_ref[...].astype(BF16)
    vac = cav_ref[...].astype(BF16)
    kan = kvan_ref[:, 0:A_KV]
    van = kvan_ref[:, A_KV:2 * A_KV]
    qa = qa_ref[...]
    q_stack = jnp.concatenate(
        [qa[:, p * 2 * HEAD_DIM:(p + 1) * 2 * HEAD_DIM] for p in range(A_GROUP)], axis=0)
    o_kv = []
    for kv in range(A_KV_HEADS):
        qm = jnp.where(lo_a, q_stack, 0) if kv == 0 else jnp.where(lo_a, 0, q_stack)
        s_c = _dot_nt(qm, kac) + bias_ac_ref[kv]
        s_n = _dot_nt(qm, kan) + bias_an_ref[kv]
        o_kv.append(_softmax_pv([s_c, s_n], [vac, van], sink_ref[kv]))
    for p in range(A_GROUP):
        o_pair = jnp.where(lo, o_kv[0][p * tn:(p + 1) * tn], o_kv[1][p * tn:(p + 1) * tn])
        oa_ref[:, p * 2 * HEAD_DIM:(p + 1) * 2 * HEAD_DIM] = o_pair.astype(BF16)
    for hp in range(B_HEADS // 2):
        c0 = hp * 2 * HEAD_DIM
        q2 = qb_ref[:, c0:c0 + 2 * HEAD_DIM]
        kc = cbk_ref[:, c0:c0 + 2 * HEAD_DIM].astype(BF16)
        vc = cbv_ref[:, c0:c0 + 2 * HEAD_DIM].astype(BF16)
        kn = kbn_ref[:, c0:c0 + 2 * HEAD_DIM]
        vn = vbn_ref[:, c0:c0 + 2 * HEAD_DIM]
        o_h = []
        for half in range(2):
            qm = jnp.where(lo, q2, 0) if half == 0 else jnp.where(lo, 0, q2)
            s_c = _dot_nt(qm, kc) + bias_bc_ref[2 * hp + half]
            s_n = _dot_nt(qm, kn) + bias_bn_ref[2 * hp + half]
            o_h.append(_softmax_pv([s_c, s_n], [vc, vn]))
        ob_ref[:, c0:c0 + 2 * HEAD_DIM] = jnp.where(lo, o_h[0], o_h[1]).astype(BF16)


def _attn_sample(qkv, cak, cav, cbk, cbv, layer, tn, bias_ac, bias_an, bias_bc, bias_bn, sink_rows):
    t = qkv.shape[0]
    nb = t // tn
    la = cak.shape[2]
    lb = cbk.shape[2]
    kva_blk = (A_Q + 3 * B_W) // (2 * A_KV)
    const3 = lambda b: (0, 0, 0)
    return pl.pallas_call(
        _attn_sample_kernel,
        grid=(nb,),
        in_specs=[
            pl.BlockSpec((tn, A_Q), lambda b: (b, 0)),
            pl.BlockSpec((tn, B_W), lambda b: (b, 1)),
            pl.BlockSpec((tn, B_W), lambda b: (b, 2)),
            pl.BlockSpec((tn, B_W), lambda b: (b, 3)),
            pl.BlockSpec((tn, 2 * A_KV), lambda b: (b, kva_blk)),
            pl.BlockSpec((None, None, la, A_KV), lambda b: (layer, b, 0, 0)),
            pl.BlockSpec((None, None, la, A_KV), lambda b: (layer, b, 0, 0)),
            pl.BlockSpec((None, None, lb, B_W), lambda b: (layer, b, 0, 0)),
            pl.BlockSpec((None, None, lb, B_W), lambda b: (layer, b, 0, 0)),
            _resident(bias_ac.shape, const3),
            _resident(bias_an.shape, const3),
            _resident(bias_bc.shape, const3),
            _resident(bias_bn.shape, const3),
            _resident(sink_rows.shape, const3),
        ],
        out_specs=[
            pl.BlockSpec((tn, A_Q), lambda b: (b, 0)),
            pl.BlockSpec((tn, B_W), lambda b: (b, 0)),
        ],
        out_shape=[jax.ShapeDtypeStruct((t, A_Q), BF16), jax.ShapeDtypeStruct((t, B_W), BF16)],
        compiler_params=_params(),
        name="attn_sample",
    )(qkv, qkv, qkv, qkv, qkv, cak, cav, cbk, cbv, bias_ac, bias_an, bias_bc, bias_bn, sink_rows)


def _merge_rows(x, oa_ref, ob_ref, gate_ref, woa_ref, wob_ref, wout_ref):
    d = x.shape[1]
    a = _dot(oa_ref[...], woa_ref[...])
    b = _dot(ob_ref[...], wob_ref[...])
    m = gate_ref[:, 0:d].astype(F32) * a + gate_ref[:, d:2 * d].astype(F32) * b
    return x + _dot(m.astype(BF16), wout_ref[...])


def _merge_kernel(x_ref, oa_ref, ob_ref, gate_ref, woa_ref, wob_ref, wout_ref, o_ref):
    o_ref[...] = _merge_rows(x_ref[...], oa_ref, ob_ref, gate_ref, woa_ref, wob_ref, wout_ref)


def _merge(x, oa, ob, gate, woa, wob, wout, tm):
    t, d = x.shape
    row = lambda i: (i, 0)
    const = lambda i: (0, 0)
    return pl.pallas_call(
        _merge_kernel,
        grid=(t // tm,),
        in_specs=[
            pl.BlockSpec((tm, d), row),
            pl.BlockSpec((tm, A_Q), row),
            pl.BlockSpec((tm, B_W), row),
            pl.BlockSpec((tm, 2 * d), row),
            _resident(woa.shape, const),
            _resident(wob.shape, const),
            _resident(wout.shape, const),
        ],
        out_specs=pl.BlockSpec((tm, d), row),
        out_shape=jax.ShapeDtypeStruct((t, d), F32),
        compiler_params=_params(),
        name="merge",
    )(x, oa, ob, gate, woa, wob, wout)


def _cross_rows(x, g_ref, wq_ref, mk_ref, mv_ref, wo_ref):
    h = _rmsnorm(x, g_ref[...]).astype(BF16)
    q = _dot(h, wq_ref[...]).astype(BF16)
    mk = mk_ref[...].astype(BF16)
    mv = mv_ref[...].astype(BF16)
    scale = X_HEAD_DIM ** -0.5 * LOG2E

    def scores(hd):
        c0 = hd * X_HEAD_DIM
        return _dot_nt(q[:, c0:c0 + X_HEAD_DIM], mk[:, c0:c0 + X_HEAD_DIM]) * scale

    pending = [scores(hd) for hd in range(min(QK_AHEAD, X_HEADS))]
    outs = []
    for hd in range(X_HEADS):
        if hd + QK_AHEAD < X_HEADS:
            pending.append(scores(hd + QK_AHEAD))
        c0 = hd * X_HEAD_DIM
        outs.append(_softmax_pv([pending.pop(0)], [mv[:, c0:c0 + X_HEAD_DIM]]))
    o = jnp.concatenate(outs, axis=-1).astype(BF16)
    return x + _dot(o, wo_ref[...])


def _cross_kernel(x_ref, g_ref, wq_ref, mk_ref, mv_ref, wo_ref, o_ref):
    o_ref[...] = _cross_rows(x_ref[...], g_ref, wq_ref, mk_ref, mv_ref, wo_ref)


def _mixout_kernel(x_ref, oa_ref, ob_ref, gate_ref, woa_ref, wob_ref, wout_ref,
                   g_ref, wq_ref, mk_ref, mv_ref, wo_ref, o_ref):
    x = _merge_rows(x_ref[...], oa_ref, ob_ref, gate_ref, woa_ref, wob_ref, wout_ref)
    o_ref[...] = _cross_rows(x, g_ref, wq_ref, mk_ref, mv_ref, wo_ref)


def _mixout(x, oa, ob, gate, woa, wob, wout, g, wq, mk, mv, layer, wo, tm):
    t, d = x.shape
    n_mem = mk.shape[-2]
    row = lambda i: (i, 0)
    const = lambda i: (0, 0)
    return pl.pallas_call(
        _mixout_kernel,
        grid=(t // tm,),
        in_specs=[
            pl.BlockSpec((tm, d), row),
            pl.BlockSpec((tm, A_Q), row),
            pl.BlockSpec((tm, B_W), row),
            pl.BlockSpec((tm, 2 * d), row),
            _resident(woa.shape, const),
            _resident(wob.shape, const),
            _resident(wout.shape, const),
            _resident((1, d), const),
            _resident(wq.shape, const),
            _resident((None, n_mem, X_W), lambda i: (layer, 0, 0)),
            _resident((None, n_mem, X_W), lambda i: (layer, 0, 0)),
            _resident(wo.shape, const),
        ],
        out_specs=pl.BlockSpec((tm, d), row),
        out_shape=jax.ShapeDtypeStruct((t, d), F32),
        compiler_params=_params(),
        name="mix_out",
    )(x, oa, ob, gate, woa, wob, wout, g.reshape(1, d), wq, mk, mv, wo)


def _cross(x3, g, wq, mk, mv, wo, tm, kv_index):
    nb, rows, d = x3.shape
    n_mem = mk.shape[-2]
    kv_block = (None,) * (mk.ndim - 2) + (n_mem, X_W)
    const = lambda b, i: (0, 0)
    return pl.pallas_call(
        _cross_kernel,
        grid=(nb, rows // tm),
        in_specs=[
            pl.BlockSpec((None, tm, d), lambda b, i: (b, i, 0)),
            _resident((1, d), const),
            _resident(wq.shape, const),
            pl.BlockSpec(kv_block, lambda b, i: kv_index(b)),
            pl.BlockSpec(kv_block, lambda b, i: kv_index(b)),
            _resident(wo.shape, const),
        ],
        out_specs=pl.BlockSpec((None, tm, d), lambda b, i: (b, i, 0)),
        out_shape=jax.ShapeDtypeStruct((nb, rows, d), F32),
        compiler_params=_params(2),
        name="cross",
    )(x3, g.reshape(1, d), wq, mk, mv, wo)


def _ffn_kernel(*refs, nb, tm, d_ff, final):
    if final:
        (x_ref, g_ref, state_ref, wup_ref, wconv_ref, bconv_ref, wdown_ref, gfin_ref,
         o_ref, ctail_ref, carry, ebuf, act) = refs
    else:
        (x_ref, g_ref, state_ref, wup_ref, wconv_ref, bconv_ref, wdown_ref,
         o_ref, ctail_ref, carry, ebuf, act) = refs
    i = pl.program_id(0)
    d = x_ref.shape[-1]
    hist = CONV_W - 1
    top = 8

    @pl.when(i == 0)
    def _():
        carry[:, top - hist:top, :] = state_ref[...]

    x = x_ref[...].reshape(nb * tm, d)
    h = _rmsnorm(x, g_ref[...]).astype(BF16)
    n_chunks = d_ff // FF_CHUNK
    for j in range(n_chunks):
        c_half = []
        for half in range(2):
            c0 = half * d_ff + j * FF_CHUNK
            slot = half * n_chunks + j
            u = _dot(h, wup_ref[:, c0:c0 + FF_CHUNK]).reshape(nb, tm, FF_CHUNK)
            ebuf[slot, :, top - hist:top, :] = carry[:, top - hist:top, c0:c0 + FF_CHUNK]
            ebuf[slot, :, top:top + tm, :] = u
            carry[:, top - hist:top, c0:c0 + FF_CHUNK] = u[:, tm - hist:tm, :]
            c = bconv_ref[:, c0:c0 + FF_CHUNK]
            for k in range(hist):
                c = c + ebuf[slot, :, top - hist + k:top - hist + k + tm, :] * wconv_ref[k:k + 1, c0:c0 + FF_CHUNK]
            c = c + u * wconv_ref[hist:hist + 1, c0:c0 + FF_CHUNK]
            c_half.append(c)
        a = jax.nn.silu(c_half[0]) * c_half[1]
        act[:, j * FF_CHUNK:(j + 1) * FF_CHUNK] = a.reshape(nb * tm, FF_CHUNK).astype(BF16)
    cuts = [0, *FF_DOWN_CUTS, n_chunks]
    y = None
    for c_lo, c_hi in zip(cuts[:-1], cuts[1:]):
        rows = slice(c_lo * FF_CHUNK, c_hi * FF_CHUNK)
        part = _dot(act[:, rows], wdown_ref[rows, :])
        y = part if y is None else y + part
    out = x + y
    if final:
        out = _rmsnorm(out, gfin_ref[...])
    o_ref[...] = out.reshape(nb, tm, d)

    @pl.when(i == pl.num_programs(0) - 1)
    def _():
        ctail_ref[...] = carry[:, top - hist:top, :]


def _ffn(x3, g, state, wup, wconv, bconv, wdown, gfin, tm):
    nb, rows, d = x3.shape
    d_ff = wdown.shape[0]
    final = gfin is not None
    const = lambda i: (0, 0)
    in_specs = [
        pl.BlockSpec((nb, tm, d), lambda i: (0, i, 0)),
        _resident((1, d), const),
        _resident(state.shape, lambda i: (0, 0, 0)),
        _resident(wup.shape, const),
        _resident(wconv.shape, const),
        _resident((1, 2 * d_ff), const),
        _resident(wdown.shape, const),
    ]
    args = [x3, g.reshape(1, d), state, wup, wconv, bconv.reshape(1, 2 * d_ff), wdown]
    if final:
        in_specs.append(_resident((1, d), const))
        args.append(gfin.reshape(1, d))
    return pl.pallas_call(
        functools.partial(_ffn_kernel, nb=nb, tm=tm, d_ff=d_ff, final=final),
        grid=(rows // tm,),
        in_specs=in_specs,
        out_specs=[
            pl.BlockSpec((nb, tm, d), lambda i: (0, i, 0)),
            pl.BlockSpec((nb, CONV_W - 1, 2 * d_ff), lambda i: (0, 0, 0)),
        ],
        out_shape=[
            jax.ShapeDtypeStruct((nb, rows, d), F32),
            jax.ShapeDtypeStruct((nb, CONV_W - 1, 2 * d_ff), F32),
        ],
        scratch_shapes=[
            pltpu.VMEM((nb, 8, 2 * d_ff), F32),
            pltpu.VMEM((2 * (d_ff // FF_CHUNK), nb, 8 + tm, FF_CHUNK), F32),
            pltpu.VMEM((nb * tm, d_ff), BF16),
        ],
        compiler_params=_params(),
        name="conv_ffn",
    )(*args)


def _alibi_slopes():
    return np.exp2(-8.0 * np.arange(1, A_HEADS + 1) / A_HEADS).astype(np.float32)


def _pair_major(w, axis):
    shp = w.shape
    w = w.reshape(shp[:axis] + (A_KV_HEADS, A_GROUP, HEAD_DIM) + shp[axis + 1:])
    w = jnp.swapaxes(w, axis, axis + 1)
    return w.reshape(shp)


def _prep_w_in(w):
    d = w.shape[1]
    c = np.cumsum([0, A_Q, A_KV, A_KV, B_W, B_W, B_W, d, d])
    qa, ka, va, qb, kb, vb, ga, gb = [w[..., c[i]:c[i + 1]] for i in range(8)]
    qscale = HEAD_DIM ** -0.5 * LOG2E
    qa = _pair_major(qa, 2) * qscale
    return jnp.concatenate([ga, gb, qa, qb * qscale, kb, vb, ka, va], axis=-1).astype(BF16)


def _band_valid(back, band, tm):
    npair = tm // PAIR
    r = np.arange(PAIR)[:, None]
    c = np.arange(band)[None, :]
    qc, kc = r // CHUNK, c // CHUNK
    in_band = (kc >= qc) & (kc <= qc + back)
    history = band - PAIR
    variants = [in_band & (j * PAIR + c >= history) for j in range(npair)] + [in_band]
    return np.where(np.stack(variants), 0.0, NEG).astype(np.float32)


def _prompt_bias_a(tm):
    r = np.arange(PAIR)[:, None]
    c = np.arange(A_BAND)[None, :]
    dist = np.abs(A_BACK * CHUNK + r - c).astype(np.float32)
    slopes = _alibi_slopes().reshape(1, A_KV_HEADS, A_GROUP, 1, 1)
    bias = (-slopes * dist + _band_valid(A_BACK, A_BAND, tm)[:, None, None]) * np.float32(LOG2E)
    return jnp.asarray(bias.reshape(-1, A_KV_HEADS, A_GROUP * PAIR, A_BAND))


def _take_clipped(table, idx):
    idx = np.asarray(idx)
    n_lo = int(np.sum(idx == idx[0])) - 1
    n_hi = int(np.sum(idx == idx[-1])) - 1 if idx[-1] != idx[0] else 0
    want = np.concatenate([np.full(n_lo, idx[0]), np.arange(idx[0], idx[-1] + 1), np.full(n_hi, idx[-1])])
    assert np.array_equal(want, idx)
    lead = table.shape[:-1]
    parts = [jnp.broadcast_to(table[..., idx[0]:idx[0] + 1], lead + (n_lo,)),
             table[..., idx[0]:idx[-1] + 1],
             jnp.broadcast_to(table[..., idx[-1]:idx[-1] + 1], lead + (n_hi,))]
    return jnp.concatenate(parts, axis=-1)


def _rel_diagonals(table, n_q, offset, width):
    idx = np.clip(np.arange(width) - (n_q - 1) + offset, -REL_CLIP, REL_CLIP) + REL_CLIP
    return _take_clipped(table, idx)


def _rel_bias_kernel(ext_ref, valid_ref, out_ref):
    w = ext_ref.shape[-1]
    for h in range(B_HEADS):
        x = jnp.broadcast_to(ext_ref[h:h + 1, :], (PAIR, w))
        y = pltpu.roll(x, w - (PAIR - 1), 1, stride=1, stride_axis=0)[:, :B_BAND]
        for j in range(valid_ref.shape[0]):
            out_ref[j, h] = y + valid_ref[j]


def _prompt_bias_b(tables, tm):
    depth = tables.shape[0]
    w = 768
    assert w % 128 == 0 and w >= PAIR + B_BAND - 1
    ext = _rel_diagonals(tables, PAIR, -B_BACK * CHUNK, w)
    valid = jnp.asarray(_band_valid(B_BACK, B_BAND, tm) * np.float32(LOG2E))
    nvar = valid.shape[0]
    return pl.pallas_call(
        _rel_bias_kernel,
        grid=(depth,),
        in_specs=[
            pl.BlockSpec((None, B_HEADS, w), lambda l: (l, 0, 0)),
            pl.BlockSpec(valid.shape, lambda l: (0, 0, 0)),
        ],
        out_specs=pl.BlockSpec((None, nvar, B_HEADS, PAIR, B_BAND), lambda l: (l, 0, 0, 0, 0)),
        out_shape=jax.ShapeDtypeStruct((depth, nvar, B_HEADS, PAIR, B_BAND), F32),
        compiler_params=_params(),
        name="rel_bias",
    )(ext, valid)


def _chunk_valid(pos_q, pos_k, back):
    cq, ck = pos_q[:, None] // CHUNK, pos_k[None, :] // CHUNK
    return np.where((ck <= cq) & (ck >= cq - back), 0.0, NEG).astype(np.float32)


def _sample_bias_a(past_len, la, tn):
    pos_q = past_len + np.arange(tn)
    slopes = _alibi_slopes().reshape(A_KV_HEADS, A_GROUP, 1, 1)

    def bias(pos_k):
        dist = np.abs(pos_q[:, None] - pos_k[None, :]).astype(np.float32)
        b = (-slopes * dist + _chunk_valid(pos_q, pos_k, A_BACK)) * np.float32(LOG2E)
        return jnp.asarray(b.reshape(A_KV_HEADS, A_GROUP * tn, pos_k.shape[0]))

    return bias(past_len - la + np.arange(la)), bias(pos_q)


def _sample_bias_b(tables, past_len, lb, tn):
    pos_q = past_len + np.arange(tn)

    def bias(pos_k, offset):
        n_k = pos_k.shape[0]
        ext = _rel_diagonals(tables, tn, offset, n_k + tn - 1)
        rows = [ext[..., tn - 1 - i:tn - 1 - i + n_k] for i in range(tn)]
        valid = _chunk_valid(pos_q, pos_k, B_BACK) * np.float32(LOG2E)
        return jnp.stack(rows, axis=-2) + valid

    return bias(past_len - lb + np.arange(lb), -lb), bias(pos_q, 0)


def _sink_rows(sink, rows):
    s = jnp.repeat(sink.reshape(A_KV_HEADS, A_GROUP, 1) * LOG2E, rows, axis=2)
    return s.reshape(A_KV_HEADS, A_GROUP * rows, 1)


def kernel(x_prompt, x_sample, mem_prompt, cache_a_k, cache_a_v, cache_b_k, cache_b_v, cache_mem_k, cache_mem_v, state_conv, g_mix, w_mix_in, a_sink, b_rel_bias, w_o_a, w_o_b, w_mix_out, g_xattn, g_mem, w_xq, w_xk, w_xv, w_xo, g_ffn, w_up, w_conv, b_conv, w_down, g_final):
    batch, seq, d = x_prompt.shape
    dec_batch, dec_seq, _ = x_sample.shape
    depth = w_mix_in.shape[0]
    n_mem = mem_prompt.shape[1]
    d_ff = w_down.shape[1]
    la, lb = cache_a_k.shape[2], cache_b_k.shape[2]
    past_len = PAST_LEN
    assert batch == 1 and seq % WIDE_TILE == 0 and d_ff % FF_CHUNK == 0
    na, nbk = min(A_BACK * CHUNK, seq), min(B_BACK * CHUNK, seq)
    assert nbk == ROW_TILE and na <= nbk
    ts = dec_batch * dec_seq

    w_in = _prep_w_in(w_mix_in)
    woa = _pair_major(w_o_a, 1).astype(BF16)
    wob = w_o_b.astype(BF16)
    wout = w_mix_out.astype(BF16)
    wxq, wxk, wxv, wxo = (w.astype(BF16) for w in (w_xq, w_xk, w_xv, w_xo))
    wup = w_up.astype(BF16)
    wdown = w_down.astype(BF16)

    rel_tables = b_rel_bias.astype(F32) * LOG2E
    bias_a_p = _prompt_bias_a(ROW_TILE)
    bias_b_p = _prompt_bias_b(rel_tables, ROW_TILE)
    bias_a_sc, bias_a_sn = _sample_bias_a(past_len, la, dec_seq)
    bias_b_sc, bias_b_sn = _sample_bias_b(rel_tables, past_len, lb, dec_seq)
    cak = cache_a_k.reshape(depth, dec_batch, la, A_KV)
    cav = cache_a_v.reshape(depth, dec_batch, la, A_KV)
    cbk = cache_b_k.reshape(depth, dec_batch, lb, B_W)
    cbv = cache_b_v.reshape(depth, dec_batch, lb, B_W)
    cmk = cache_mem_k.reshape(depth, dec_batch, n_mem, X_W)
    cmv = cache_mem_v.reshape(depth, dec_batch, n_mem, X_W)
    zero_state = jnp.zeros((batch, CONV_W - 1, 2 * d_ff), F32)

    pm_k, pm_v = _memkv(mem_prompt[0], g_mem, wxk, wxv)
    pm_k_bf, pm_v_bf = pm_k.astype(BF16), pm_v.astype(BF16)

    xp = x_prompt.reshape(seq, d)
    xs = x_sample.reshape(ts, d)
    outs = {k: [] for k in ("pa_k", "pa_v", "pb_k", "pb_v", "pconv", "sa_k", "sa_v", "sb_k", "sb_v", "sconv")}
    for l in range(depth):
        gfin = g_final if l == depth - 1 else None
        qkv, gate, tail = _inproj(xp, g_mix[l], w_in[l], WIDE_TILE, nbk)
        oa, ob = _attn_prompt(qkv, bias_a_p, bias_b_p, l, _sink_rows(a_sink[l], PAIR))
        xp = _mixout(xp, oa, ob, gate, woa[l], wob[l], wout[l], g_xattn[l], wxq[l], pm_k_bf, pm_v_bf, l,
                     wxo[l], WIDE_TILE)
        xp, cp = _ffn(xp.reshape(batch, seq, d), g_ffn[l], zero_state, wup[l], w_conv[l], b_conv[l],
                      wdown[l], gfin, ROW_TILE)
        xp = xp.reshape(seq, d)
        outs["pb_k"].append(tail[:, 0:B_W].reshape(batch, nbk, B_HEADS, HEAD_DIM))
        outs["pb_v"].append(tail[:, B_W:2 * B_W].reshape(batch, nbk, B_HEADS, HEAD_DIM))
        outs["pa_k"].append(tail[nbk - na:, 2 * B_W:2 * B_W + A_KV].reshape(batch, na, A_KV_HEADS, HEAD_DIM))
        outs["pa_v"].append(tail[nbk - na:, 2 * B_W + A_KV:].reshape(batch, na, A_KV_HEADS, HEAD_DIM))
        outs["pconv"].append(cp)
        qkv, gate, tail = _inproj(xs, g_mix[l], w_in[l], ts, ts)
        oa, ob = _attn_sample(qkv, cak, cav, cbk, cbv, l, dec_seq, bias_a_sc, bias_a_sn,
                              bias_b_sc[l], bias_b_sn[l], _sink_rows(a_sink[l], dec_seq))
        xs = _merge(xs, oa, ob, gate, woa[l], wob[l], wout[l], ts)
        xs = _cross(xs.reshape(dec_batch, dec_seq, d), g_xattn[l], wxq[l], cmk, cmv, wxo[l], dec_seq,
                    lambda b, l=l: (l, b, 0, 0))
        xs, cs = _ffn(xs, g_ffn[l], state_conv[l], wup[l], w_conv[l], b_conv[l], wdown[l], gfin, dec_seq)
        xs = xs.reshape(ts, d)
        outs["sb_k"].append(tail[:, 0:B_W].reshape(dec_batch, dec_seq, B_HEADS, HEAD_DIM))
        outs["sb_v"].append(tail[:, B_W:2 * B_W].reshape(dec_batch, dec_seq, B_HEADS, HEAD_DIM))
        outs["sa_k"].append(tail[:, 2 * B_W:2 * B_W + A_KV].reshape(dec_batch, dec_seq, A_KV_HEADS, HEAD_DIM))
        outs["sa_v"].append(tail[:, 2 * B_W + A_KV:].reshape(dec_batch, dec_seq, A_KV_HEADS, HEAD_DIM))
        outs["sconv"].append(cs)

    st = {k: jnp.stack(v) for k, v in outs.items()}
    pm_k = pm_k.reshape(depth, batch, n_mem, X_HEADS, X_HEAD_DIM)
    pm_v = pm_v.reshape(depth, batch, n_mem, X_HEADS, X_HEAD_DIM)
    return (xp.reshape(batch, seq, d), xs.reshape(dec_batch, dec_seq, d),
            st["pa_k"], st["pa_v"], st["pb_k"], st["pb_v"], pm_k, pm_v, st["pconv"],
            st["sa_k"], st["sa_v"], st["sb_k"], st["sb_v"], st["sconv"])
```

```python
import functools

import numpy as np
import jax
import jax.numpy as jnp
from jax import lax
from jax.experimental import pallas as pl
from jax.experimental.pallas import tpu as pltpu

CHUNK = 64
HEAD_DIM = 64
A_HEADS = 8
A_KV_HEADS = 2
A_GROUP = A_HEADS // A_KV_HEADS
A_BACK = 2
B_HEADS = 8
B_BACK = 8
REL_CLIP = 256
X_HEADS = 4
X_HEAD_DIM = 128
EPS = 1e-6
NEG = -1e30
LOG2E = float(np.log2(np.e))

A_Q = A_HEADS * HEAD_DIM
A_KV = A_KV_HEADS * HEAD_DIM
B_W = B_HEADS * HEAD_DIM
X_W = X_HEADS * X_HEAD_DIM
CONV_W = 3
PAST_LEN = 1024

PAIR = 2 * CHUNK
ROW_TILE = 512
WIDE_TILE = 1024
A_BAND = (A_BACK + 2) * CHUNK
B_BAND = (B_BACK + 2) * CHUNK
FF_CHUNK = 256
FF_DOWN_CUTS = (8,)
QK_AHEAD = 2

LANES = 128
V7X_VMEM_LIMIT = 56 * 1024 * 1024

F32 = jnp.float32
BF16 = jnp.bfloat16


def _params(n_axes=1):
    return pltpu.CompilerParams(
        dimension_semantics=("arbitrary",) * n_axes,
        vmem_limit_bytes=V7X_VMEM_LIMIT)


def _resident(shape, index_map):
    return pl.BlockSpec(shape, index_map, pipeline_mode=pl.Buffered(1))


def _rmsnorm(x, g):
    ms = jnp.mean(x * x, axis=-1, keepdims=True)
    return (x * lax.rsqrt(ms + EPS)) * g


def _dot(a, b):
    return jnp.dot(a, b, preferred_element_type=F32)


def _dot_nt(a, b):
    return lax.dot_general(a, b, (((1,), (1,)), ((), ())), preferred_element_type=F32)


def _softmax_pv(s_list, v_list, sink=None):
    m = s_list[0].max(axis=-1, keepdims=True)
    for s in s_list[1:]:
        m = jnp.maximum(m, s.max(axis=-1, keepdims=True))
    if sink is not None:
        m = jnp.maximum(m, sink)
    l = None
    o = None
    for s, v in zip(s_list, v_list):
        p = jnp.exp2(s - m)
        ls = p.sum(axis=-1, keepdims=True)
        os_ = _dot(p.astype(BF16), v)
        l = ls if l is None else l + ls
        o = os_ if o is None else o + os_
    if sink is not None:
        l = l + jnp.exp2(sink - m)
    return o * (1.0 / l)


def _lane_lo(shape):
    return lax.broadcasted_iota(jnp.int32, shape, len(shape) - 1) < HEAD_DIM


def _memkv_kernel(mem_ref, g_ref, wk_ref, wv_ref, k_ref, v_ref):
    hm = _rmsnorm(mem_ref[...], g_ref[...]).astype(BF16)
    k_ref[...] = _dot(hm, wk_ref[...])
    v_ref[...] = _dot(hm, wv_ref[...])


def _memkv(mem, g_mem, w_xk, w_xv):
    depth, d, xw = w_xk.shape
    n_mem = mem.shape[0]
    return pl.pallas_call(
        _memkv_kernel,
        grid=(depth,),
        in_specs=[
            pl.BlockSpec((n_mem, d), lambda l: (0, 0)),
            pl.BlockSpec((None, 1, d), lambda l: (l, 0, 0)),
            pl.BlockSpec((None, d, xw), lambda l: (l, 0, 0)),
            pl.BlockSpec((None, d, xw), lambda l: (l, 0, 0)),
        ],
        out_specs=[
            pl.BlockSpec((None, n_mem, xw), lambda l: (l, 0, 0)),
            pl.BlockSpec((None, n_mem, xw), lambda l: (l, 0, 0)),
        ],
        out_shape=[jax.ShapeDtypeStruct((depth, n_mem, xw), F32)] * 2,
        compiler_params=_params(),
        name="mem_kv",
    )(mem, g_mem.reshape(depth, 1, d), w_xk, w_xv)


N_QKV = A_Q + 3 * B_W + 2 * A_KV
N_TAIL = 2 * B_W + 2 * A_KV


def _inproj_kernel(x_ref, g_ref, w_ref, qkv_ref, gate_ref, tail_ref):
    h = _rmsnorm(x_ref[...], g_ref[...]).astype(BF16)
    n_gate = gate_ref.shape[1]
    gate_ref[...] = jax.nn.sigmoid(_dot(h, w_ref[:, :n_gate])).astype(BF16)
    z = _dot(h, w_ref[:, n_gate:])
    qkv_ref[...] = z.astype(BF16)

    @pl.when(pl.program_id(0) == pl.num_programs(0) - 1)
    def _():
        tail_ref[...] = z[z.shape[0] - tail_ref.shape[0]:, N_QKV - N_TAIL:]


def _inproj(x, g, w, tm, tail_rows):
    t, d = x.shape
    n = w.shape[1]
    n_gate = 2 * d
    assert tail_rows <= tm and n == n_gate + N_QKV
    return pl.pallas_call(
        _inproj_kernel,
        grid=(t // tm,),
        in_specs=[
            pl.BlockSpec((tm, d), lambda i: (i, 0)),
            _resident((1, d), lambda i: (0, 0)),
            _resident((d, n), lambda i: (0, 0)),
        ],
        out_specs=[
            pl.BlockSpec((tm, N_QKV), lambda i: (i, 0)),
            pl.BlockSpec((tm, n_gate), lambda i: (i, 0)),
            pl.BlockSpec((tail_rows, N_TAIL), lambda i: (0, 0)),
        ],
        out_shape=[
            jax.ShapeDtypeStruct((t, N_QKV), BF16),
            jax.ShapeDtypeStruct((t, n_gate), BF16),
            jax.ShapeDtypeStruct((tail_rows, N_TAIL), F32),
        ],
        compiler_params=_params(),
        name="inproj",
    )(x, g.reshape(1, d), w)


def _attn_prompt_kernel(qa_ref, qb_ref, kbc_ref, kbp_ref, vbc_ref, vbp_ref,
                        kvac_ref, kvap_ref, bias_a_ref, bias_b_ref, sink_ref,
                        oa_ref, ob_ref, kcat, vcat, kvacat):
    tm = qa_ref.shape[0]
    npair = tm // PAIR
    first_tile = pl.program_id(0) == 0
    kcat[0:tm, :] = kbp_ref[...]
    kcat[tm:2 * tm, :] = kbc_ref[...]
    vcat[0:tm, :] = vbp_ref[...]
    vcat[tm:2 * tm, :] = vbc_ref[...]
    kvacat[0:PAIR, :] = kvap_ref[...]
    kvacat[PAIR:PAIR + tm, :] = kvac_ref[...]
    lo = _lane_lo((PAIR, 2 * HEAD_DIM))
    lo_a = _lane_lo((A_GROUP * PAIR, 2 * HEAD_DIM))

    def pair_body(d, carry):
        r0 = pl.multiple_of(d * PAIR, PAIR)
        var = jnp.where(first_tile, d, npair)
        ka2 = kvacat[pl.ds(r0, A_BAND), 0:A_KV]
        va2 = kvacat[pl.ds(r0, A_BAND), A_KV:2 * A_KV]
        qa = qa_ref[pl.ds(r0, PAIR), :]
        q_stack = jnp.concatenate(
            [qa[:, p * 2 * HEAD_DIM:(p + 1) * 2 * HEAD_DIM] for p in range(A_GROUP)], axis=0)
        b_operands = {}

        def b_group(hp):
            if hp not in b_operands:
                c0 = hp * 2 * HEAD_DIM
                b_operands[hp] = (qb_ref[pl.ds(r0, PAIR), c0:c0 + 2 * HEAD_DIM],
                                  kcat[pl.ds(r0, B_BAND), c0:c0 + 2 * HEAD_DIM],
                                  vcat[pl.ds(r0, B_BAND), c0:c0 + 2 * HEAD_DIM])
            return b_operands[hp]

        def scores(stage):
            if stage[0] == "a":
                kv = stage[1]
                qm = jnp.where(lo_a, q_stack, 0) if kv == 0 else jnp.where(lo_a, 0, q_stack)
                return _dot_nt(qm, ka2) + bias_a_ref[var, kv]
            _, hp, half = stage
            q2, k2, _ = b_group(hp)
            qm = jnp.where(lo, q2, 0) if half == 0 else jnp.where(lo, 0, q2)
            return _dot_nt(qm, k2) + bias_b_ref[var, 2 * hp + half]

        stages = ([("a", kv) for kv in range(A_KV_HEADS)]
                  + [("b", hp, half) for hp in range(B_HEADS // 2) for half in range(2)])
        done = {}
        pending = [scores(st) for st in stages[:QK_AHEAD]]
        for n, stage in enumerate(stages):
            if n + QK_AHEAD < len(stages):
                pending.append(scores(stages[n + QK_AHEAD]))
            s = pending.pop(0)
            if stage[0] == "a":
                done[stage] = _softmax_pv([s], [va2], sink_ref[stage[1]])
                if stage[1] == A_KV_HEADS - 1:
                    for p in range(A_GROUP):
                        rows = slice(p * PAIR, (p + 1) * PAIR)
                        o_pair = jnp.where(lo, done[("a", 0)][rows], done[("a", 1)][rows])
                        oa_ref[pl.ds(r0, PAIR), p * 2 * HEAD_DIM:(p + 1) * 2 * HEAD_DIM] = o_pair.astype(BF16)
            else:
                _, hp, half = stage
                done[stage] = _softmax_pv([s], [b_group(hp)[2]])
                if half == 1:
                    c0 = hp * 2 * HEAD_DIM
                    ob_ref[pl.ds(r0, PAIR), c0:c0 + 2 * HEAD_DIM] = jnp.where(
                        lo, done[("b", hp, 0)], done[("b", hp, 1)]).astype(BF16)
        return carry

    lax.fori_loop(0, tm // PAIR, pair_body, 0)


def _attn_prompt(qkv, bias_a, bias_b, layer, sink_rows):
    t = qkv.shape[0]
    tm = ROW_TILE
    npair = tm // PAIR
    prev = lambda i: jnp.maximum(i - 1, 0)
    return pl.pallas_call(
        _attn_prompt_kernel,
        grid=(t // tm,),
        in_specs=[
            pl.BlockSpec((tm, A_Q), lambda i: (i, 0)),
            pl.BlockSpec((tm, B_W), lambda i: (i, 1)),
            pl.BlockSpec((tm, B_W), lambda i: (i, 2)),
            pl.BlockSpec((tm, B_W), lambda i: (prev(i), 2)),
            pl.BlockSpec((tm, B_W), lambda i: (i, 3)),
            pl.BlockSpec((tm, B_W), lambda i: (prev(i), 3)),
            pl.BlockSpec((tm, 2 * A_KV), lambda i: (i, (A_Q + 3 * B_W) // (2 * A_KV))),
            pl.BlockSpec((PAIR, 2 * A_KV),
                         lambda i: (jnp.maximum(i * npair - 1, 0), (A_Q + 3 * B_W) // (2 * A_KV))),
            _resident(bias_a.shape, lambda i: (0, 0, 0, 0)),
            _resident((None,) + bias_b.shape[1:], lambda i: (layer, 0, 0, 0, 0)),
            _resident(sink_rows.shape, lambda i: (0, 0, 0)),
        ],
        out_specs=[
            pl.BlockSpec((tm, A_Q), lambda i: (i, 0)),
            pl.BlockSpec((tm, B_W), lambda i: (i, 0)),
        ],
        out_shape=[jax.ShapeDtypeStruct((t, A_Q), BF16), jax.ShapeDtypeStruct((t, B_W), BF16)],
        scratch_shapes=[
            pltpu.VMEM((2 * tm, B_W), BF16),
            pltpu.VMEM((2 * tm, B_W), BF16),
            pltpu.VMEM((PAIR + tm, 2 * A_KV), BF16),
        ],
        compiler_params=_params(),
        name="attn_prompt",
    )(qkv, qkv, qkv, qkv, qkv, qkv, qkv, qkv, bias_a, bias_b, sink_rows)


def _attn_sample_kernel(qa_ref, qb_ref, kbn_ref, vbn_ref, kvan_ref,
                        cak_ref, cav_ref, cbk_ref, cbv_ref,
                        bias_ac_ref, bias_an_ref, bias_bc_ref, bias_bn_ref, sink_ref,
                        oa_ref, ob_ref):
    tn = qa_ref.shape[0]
    lo = _lane_lo((tn, 2 * HEAD_DIM))
    lo_a = _lane_lo((A_GROUP * tn, 2 * HEAD_DIM))
    kac = cak_ref[...].astype(BF16)
    vac = cav_ref[...].astype(BF16)
    kan = kvan_ref[:, 0:A_KV]
    van = kvan_ref[:, A_KV:2 * A_KV]
    qa = qa_ref[...]
    q_stack = jnp.concatenate(
        [qa[:, p * 2 * HEAD_DIM:(p + 1) * 2 * HEAD_DIM] for p in range(A_GROUP)], axis=0)
    b_operands = {}

    def b_group(hp):
        if hp not in b_operands:
            cols = slice(hp * 2 * HEAD_DIM, (hp + 1) * 2 * HEAD_DIM)
            b_operands[hp] = (qb_ref[:, cols], cbk_ref[:, cols].astype(BF16), kbn_ref[:, cols],
                              cbv_ref[:, cols].astype(BF16), vbn_ref[:, cols])
        return b_operands[hp]

    def scores(stage):
        if stage[0] == "a":
            kv = stage[1]
            qm = jnp.where(lo_a, q_stack, 0) if kv == 0 else jnp.where(lo_a, 0, q_stack)
            return [_dot_nt(qm, kac) + bias_ac_ref[kv], _dot_nt(qm, kan) + bias_an_ref[kv]]
        _, hp, half = stage
        q2, kc, kn, _, _ = b_group(hp)
        qm = jnp.where(lo, q2, 0) if half == 0 else jnp.where(lo, 0, q2)
        return [_dot_nt(qm, kc) + bias_bc_ref[2 * hp + half], _dot_nt(qm, kn) + bias_bn_ref[2 * hp + half]]

    stages = ([("a", kv) for kv in range(A_KV_HEADS)]
              + [("b", hp, half) for hp in range(B_HEADS // 2) for half in range(2)])
    done = {}
    pending = [scores(st) for st in stages[:QK_AHEAD]]
    for n, stage in enumerate(stages):
        if n + QK_AHEAD < len(stages):
            pending.append(scores(stages[n + QK_AHEAD]))
        s = pending.pop(0)
        if stage[0] == "a":
            done[stage] = _softmax_pv(s, [vac, van], sink_ref[stage[1]])
            if stage[1] == A_KV_HEADS - 1:
                for p in range(A_GROUP):
                    rows = slice(p * tn, (p + 1) * tn)
                    o_pair = jnp.where(lo, done[("a", 0)][rows], done[("a", 1)][rows])
                    oa_ref[:, p * 2 * HEAD_DIM:(p + 1) * 2 * HEAD_DIM] = o_pair.astype(BF16)
        else:
            _, hp, half = stage
            done[stage] = _softmax_pv(s, list(b_group(hp)[3:5]))
            if half == 1:
                c0 = hp * 2 * HEAD_DIM
                ob_ref[:, c0:c0 + 2 * HEAD_DIM] = jnp.where(
                    lo, done[("b", hp, 0)], done[("b", hp, 1)]).astype(BF16)


def _attn_sample(qkv, cak, cav, cbk, cbv, layer, tn, bias_ac, bias_an, bias_bc, bias_bn, sink_rows):
    t = qkv.shape[0]
    nb = t // tn
    la = cak.shape[2]
    lb = cbk.shape[2]
    kva_blk = (A_Q + 3 * B_W) // (2 * A_KV)
    const3 = lambda b: (0, 0, 0)
    return pl.pallas_call(
        _attn_sample_kernel,
        grid=(nb,),
        in_specs=[
            pl.BlockSpec((tn, A_Q), lambda b: (b, 0)),
            pl.BlockSpec((tn, B_W), lambda b: (b, 1)),
            pl.BlockSpec((tn, B_W), lambda b: (b, 2)),
            pl.BlockSpec((tn, B_W), lambda b: (b, 3)),
            pl.BlockSpec((tn, 2 * A_KV), lambda b: (b, kva_blk)),
            pl.BlockSpec((None, None, la, A_KV), lambda b: (layer, b, 0, 0)),
            pl.BlockSpec((None, None, la, A_KV), lambda b: (layer, b, 0, 0)),
            pl.BlockSpec((None, None, lb, B_W), lambda b: (layer, b, 0, 0)),
            pl.BlockSpec((None, None, lb, B_W), lambda b: (layer, b, 0, 0)),
            _resident(bias_ac.shape, const3),
            _resident(bias_an.shape, const3),
            _resident(bias_bc.shape, const3),
            _resident(bias_bn.shape, const3),
            _resident(sink_rows.shape, const3),
        ],
        out_specs=[
            pl.BlockSpec((tn, A_Q), lambda b: (b, 0)),
            pl.BlockSpec((tn, B_W), lambda b: (b, 0)),
        ],
        out_shape=[jax.ShapeDtypeStruct((t, A_Q), BF16), jax.ShapeDtypeStruct((t, B_W), BF16)],
        compiler_params=_params(),
        name="attn_sample",
    )(qkv, qkv, qkv, qkv, qkv, cak, cav, cbk, cbv, bias_ac, bias_an, bias_bc, bias_bn, sink_rows)


def _merge_rows(x, oa_ref, ob_ref, gate_ref, woa_ref, wob_ref, wout_ref):
    d = x.shape[1]
    a = _dot(oa_ref[...], woa_ref[...])
    b = _dot(ob_ref[...], wob_ref[...])
    m = gate_ref[:, 0:d].astype(F32) * a + gate_ref[:, d:2 * d].astype(F32) * b
    return x + _dot(m.astype(BF16), wout_ref[...])


def _merge_kernel(x_ref, oa_ref, ob_ref, gate_ref, woa_ref, wob_ref, wout_ref, o_ref):
    o_ref[...] = _merge_rows(x_ref[...], oa_ref, ob_ref, gate_ref, woa_ref, wob_ref, wout_ref)


def _merge(x, oa, ob, gate, woa, wob, wout, tm):
    t, d = x.shape
    row = lambda i: (i, 0)
    const = lambda i: (0, 0)
    return pl.pallas_call(
        _merge_kernel,
        grid=(t // tm,),
        in_specs=[
            pl.BlockSpec((tm, d), row),
            pl.BlockSpec((tm, A_Q), row),
            pl.BlockSpec((tm, B_W), row),
            pl.BlockSpec((tm, 2 * d), row),
            _resident(woa.shape, const),
            _resident(wob.shape, const),
            _resident(wout.shape, const),
        ],
        out_specs=pl.BlockSpec((tm, d), row),
        out_shape=jax.ShapeDtypeStruct((t, d), F32),
        compiler_params=_params(),
        name="merge",
    )(x, oa, ob, gate, woa, wob, wout)


def _cross_rows(x, g_ref, wq_ref, mk_ref, mv_ref, wo_ref):
    h = _rmsnorm(x, g_ref[...]).astype(BF16)
    q = _dot(h, wq_ref[...]).astype(BF16)
    mk = mk_ref[...].astype(BF16)
    mv = mv_ref[...].astype(BF16)
    scale = X_HEAD_DIM ** -0.5 * LOG2E

    def scores(hd):
        c0 = hd * X_HEAD_DIM
        return _dot_nt(q[:, c0:c0 + X_HEAD_DIM], mk[:, c0:c0 + X_HEAD_DIM]) * scale

    pending = [scores(hd) for hd in range(min(QK_AHEAD, X_HEADS))]
    outs = []
    for hd in range(X_HEADS):
        if hd + QK_AHEAD < X_HEADS:
            pending.append(scores(hd + QK_AHEAD))
        c0 = hd * X_HEAD_DIM
        outs.append(_softmax_pv([pending.pop(0)], [mv[:, c0:c0 + X_HEAD_DIM]]))
    o = jnp.concatenate(outs, axis=-1).astype(BF16)
    return x + _dot(o, wo_ref[...])


def _cross_kernel(x_ref, g_ref, wq_ref, mk_ref, mv_ref, wo_ref, o_ref):
    o_ref[...] = _cross_rows(x_ref[...], g_ref, wq_ref, mk_ref, mv_ref, wo_ref)


def _mixout_kernel(x_ref, oa_ref, ob_ref, gate_ref, woa_ref, wob_ref, wout_ref,
                   g_ref, wq_ref, mk_ref, mv_ref, wo_ref, o_ref):
    x = _merge_rows(x_ref[...], oa_ref, ob_ref, gate_ref, woa_ref, wob_ref, wout_ref)
    o_ref[...] = _cross_rows(x, g_ref, wq_ref, mk_ref, mv_ref, wo_ref)


def _mixout(x, oa, ob, gate, woa, wob, wout, g, wq, mk, mv, layer, wo, tm):
    t, d = x.shape
    n_mem = mk.shape[-2]
    row = lambda i: (i, 0)
    const = lambda i: (0, 0)
    return pl.pallas_call(
        _mixout_kernel,
        grid=(t // tm,),
        in_specs=[
            pl.BlockSpec((tm, d), row),
            pl.BlockSpec((tm, A_Q), row),
            pl.BlockSpec((tm, B_W), row),
            pl.BlockSpec((tm, 2 * d), row),
            _resident(woa.shape, const),
            _resident(wob.shape, const),
            _resident(wout.shape, const),
            _resident((1, d), const),
            _resident(wq.shape, const),
            _resident((None, n_mem, X_W), lambda i: (layer, 0, 0)),
            _resident((None, n_mem, X_W), lambda i: (layer, 0, 0)),
            _resident(wo.shape, const),
        ],
        out_specs=pl.BlockSpec((tm, d), row),
        out_shape=jax.ShapeDtypeStruct((t, d), F32),
        compiler_params=_params(),
        name="mix_out",
    )(x, oa, ob, gate, woa, wob, wout, g.reshape(1, d), wq, mk, mv, wo)


def _cross(x3, g, wq, mk, mv, wo, tm, kv_index):
    nb, rows, d = x3.shape
    n_mem = mk.shape[-2]
    kv_block = (None,) * (mk.ndim - 2) + (n_mem, X_W)
    const = lambda b, i: (0, 0)
    return pl.pallas_call(
        _cross_kernel,
        grid=(nb, rows // tm),
        in_specs=[
            pl.BlockSpec((None, tm, d), lambda b, i: (b, i, 0)),
            _resident((1, d), const),
            _resident(wq.shape, const),
            pl.BlockSpec(kv_block, lambda b, i: kv_index(b)),
            pl.BlockSpec(kv_block, lambda b, i: kv_index(b)),
            _resident(wo.shape, const),
        ],
        out_specs=pl.BlockSpec((None, tm, d), lambda b, i: (b, i, 0)),
        out_shape=jax.ShapeDtypeStruct((nb, rows, d), F32),
        compiler_params=_params(2),
        name="cross",
    )(x3, g.reshape(1, d), wq, mk, mv, wo)


def _ffn_kernel(*refs, nb, tm, d_ff, final):
    if final:
        (x_ref, g_ref, state_ref, wup_ref, wconv_ref, bconv_ref, wdown_ref, gfin_ref,
         o_ref, ctail_ref, carry, ebuf, act) = refs
    else:
        (x_ref, g_ref, state_ref, wup_ref, wconv_ref, bconv_ref, wdown_ref,
         o_ref, ctail_ref, carry, ebuf, act) = refs
    i = pl.program_id(0)
    d = x_ref.shape[-1]
    hist = CONV_W - 1
    top = 8

    @pl.when(i == 0)
    def _():
        carry[:, top - hist:top, :] = state_ref[...]

    x = x_ref[...].reshape(nb * tm, d)
    h = _rmsnorm(x, g_ref[...]).astype(BF16)
    n_chunks = d_ff // FF_CHUNK
    for j in range(n_chunks):
        c_half = []
        for half in range(2):
            c0 = half * d_ff + j * FF_CHUNK
            slot = half * n_chunks + j
            u = _dot(h, wup_ref[:, c0:c0 + FF_CHUNK]).reshape(nb, tm, FF_CHUNK)
            ebuf[slot, :, top - hist:top, :] = carry[:, top - hist:top, c0:c0 + FF_CHUNK]
            ebuf[slot, :, top:top + tm, :] = u
            carry[:, top - hist:top, c0:c0 + FF_CHUNK] = u[:, tm - hist:tm, :]
            c = bconv_ref[:, c0:c0 + FF_CHUNK]
            for k in range(hist):
                c = c + ebuf[slot, :, top - hist + k:top - hist + k + tm, :] * wconv_ref[k:k + 1, c0:c0 + FF_CHUNK]
            c = c + u * wconv_ref[hist:hist + 1, c0:c0 + FF_CHUNK]
            c_half.append(c)
        a = jax.nn.silu(c_half[0]) * c_half[1]
        act[:, j * FF_CHUNK:(j + 1) * FF_CHUNK] = a.reshape(nb * tm, FF_CHUNK).astype(BF16)
    cuts = [0, *FF_DOWN_CUTS, n_chunks]
    y = None
    for c_lo, c_hi in zip(cuts[:-1], cuts[1:]):
        rows = slice(c_lo * FF_CHUNK, c_hi * FF_CHUNK)
        part = _dot(act[:, rows], wdown_ref[rows, :])
        y = part if y is None else y + part
    out = x + y
    if final:
        out = _rmsnorm(out, gfin_ref[...])
    o_ref[...] = out.reshape(nb, tm, d)

    @pl.when(i == pl.num_programs(0) - 1)
    def _():
        ctail_ref[...] = carry[:, top - hist:top, :]


def _ffn(x3, g, state, wup, wconv, bconv, wdown, gfin, tm):
    nb, rows, d = x3.shape
    d_ff = wdown.shape[0]
    final = gfin is not None
    const = lambda i: (0, 0)
    in_specs = [
        pl.BlockSpec((nb, tm, d), lambda i: (0, i, 0)),
        _resident((1, d), const),
        _resident(state.shape, lambda i: (0, 0, 0)),
        _resident(wup.shape, const),
        _resident(wconv.shape, const),
        _resident((1, 2 * d_ff), const),
        _resident(wdown.shape, const),
    ]
    args = [x3, g.reshape(1, d), state, wup, wconv, bconv.reshape(1, 2 * d_ff), wdown]
    if final:
        in_specs.append(_resident((1, d), const))
        args.append(gfin.reshape(1, d))
    return pl.pallas_call(
        functools.partial(_ffn_kernel, nb=nb, tm=tm, d_ff=d_ff, final=final),
        grid=(rows // tm,),
        in_specs=in_specs,
        out_specs=[
            pl.BlockSpec((nb, tm, d), lambda i: (0, i, 0)),
            pl.BlockSpec((nb, CONV_W - 1, 2 * d_ff), lambda i: (0, 0, 0)),
        ],
        out_shape=[
            jax.ShapeDtypeStruct((nb, rows, d), F32),
            jax.ShapeDtypeStruct((nb, CONV_W - 1, 2 * d_ff), F32),
        ],
        scratch_shapes=[
            pltpu.VMEM((nb, 8, 2 * d_ff), F32),
            pltpu.VMEM((2 * (d_ff // FF_CHUNK), nb, 8 + tm, FF_CHUNK), F32),
            pltpu.VMEM((nb * tm, d_ff), BF16),
        ],
        compiler_params=_params(),
        name="conv_ffn",
    )(*args)


def _alibi_slopes():
    return np.exp2(-8.0 * np.arange(1, A_HEADS + 1) / A_HEADS).astype(np.float32)


def _pair_major(w, axis):
    shp = w.shape
    w = w.reshape(shp[:axis] + (A_KV_HEADS, A_GROUP, HEAD_DIM) + shp[axis + 1:])
    w = jnp.swapaxes(w, axis, axis + 1)
    return w.reshape(shp)


def _prep_w_in(w):
    d = w.shape[1]
    c = np.cumsum([0, A_Q, A_KV, A_KV, B_W, B_W, B_W, d, d])
    qa, ka, va, qb, kb, vb, ga, gb = [w[..., c[i]:c[i + 1]] for i in range(8)]
    qscale = HEAD_DIM ** -0.5 * LOG2E
    qa = _pair_major(qa, 2) * qscale
    return jnp.concatenate([ga, gb, qa, qb * qscale, kb, vb, ka, va], axis=-1).astype(BF16)


def _band_valid(back, band, tm):
    npair = tm // PAIR
    r = np.arange(PAIR)[:, None]
    c = np.arange(band)[None, :]
    qc, kc = r // CHUNK, c // CHUNK
    in_band = (kc >= qc) & (kc <= qc + back)
    history = band - PAIR
    variants = [in_band & (j * PAIR + c >= history) for j in range(npair)] + [in_band]
    return np.where(np.stack(variants), 0.0, NEG).astype(np.float32)


def _prompt_bias_a(tm):
    r = np.arange(PAIR)[:, None]
    c = np.arange(A_BAND)[None, :]
    dist = np.abs(A_BACK * CHUNK + r - c).astype(np.float32)
    slopes = _alibi_slopes().reshape(1, A_KV_HEADS, A_GROUP, 1, 1)
    bias = (-slopes * dist + _band_valid(A_BACK, A_BAND, tm)[:, None, None]) * np.float32(LOG2E)
    return jnp.asarray(bias.reshape(-1, A_KV_HEADS, A_GROUP * PAIR, A_BAND))


def _take_clipped(table, idx):
    idx = np.asarray(idx)
    n_lo = int(np.sum(idx == idx[0])) - 1
    n_hi = int(np.sum(idx == idx[-1])) - 1 if idx[-1] != idx[0] else 0
    want = np.concatenate([np.full(n_lo, idx[0]), np.arange(idx[0], idx[-1] + 1), np.full(n_hi, idx[-1])])
    assert np.array_equal(want, idx)
    lead = table.shape[:-1]
    parts = [jnp.broadcast_to(table[..., idx[0]:idx[0] + 1], lead + (n_lo,)),
             table[..., idx[0]:idx[-1] + 1],
             jnp.broadcast_to(table[..., idx[-1]:idx[-1] + 1], lead + (n_hi,))]
    return jnp.concatenate(parts, axis=-1)


def _rel_diagonals(table, n_q, offset, width):
    idx = np.clip(np.arange(width) - (n_q - 1) + offset, -REL_CLIP, REL_CLIP) + REL_CLIP
    return _take_clipped(table, idx)


def _rel_bias_kernel(ext_ref, valid_ref, out_ref):
    w = ext_ref.shape[-1]
    for h in range(B_HEADS):
        x = jnp.broadcast_to(ext_ref[h:h + 1, :], (PAIR, w))
        y = pltpu.roll(x, w - (PAIR - 1), 1, stride=1, stride_axis=0)[:, :B_BAND]
        for j in range(valid_ref.shape[0]):
            out_ref[j, h] = y + valid_ref[j]


def _prompt_bias_b(tables, tm):
    depth = tables.shape[0]
    w = -(-(PAIR + B_BAND - 1) // LANES) * LANES
    ext = _rel_diagonals(tables, PAIR, -B_BACK * CHUNK, w)
    valid = jnp.asarray(_band_valid(B_BACK, B_BAND, tm) * np.float32(LOG2E))
    nvar = valid.shape[0]
    return pl.pallas_call(
        _rel_bias_kernel,
        grid=(depth,),
        in_specs=[
            pl.BlockSpec((None, B_HEADS, w), lambda l: (l, 0, 0)),
            pl.BlockSpec(valid.shape, lambda l: (0, 0, 0)),
        ],
        out_specs=pl.BlockSpec((None, nvar, B_HEADS, PAIR, B_BAND), lambda l: (l, 0, 0, 0, 0)),
        out_shape=jax.ShapeDtypeStruct((depth, nvar, B_HEADS, PAIR, B_BAND), F32),
        compiler_params=_params(),
        name="rel_bias",
    )(ext, valid)


def _chunk_valid(pos_q, pos_k, back):
    cq, ck = pos_q[:, None] // CHUNK, pos_k[None, :] // CHUNK
    return np.where((ck <= cq) & (ck >= cq - back), 0.0, NEG).astype(np.float32)


def _sample_bias_a(past_len, la, tn):
    pos_q = past_len + np.arange(tn)
    slopes = _alibi_slopes().reshape(A_KV_HEADS, A_GROUP, 1, 1)

    def bias(pos_k):
        dist = np.abs(pos_q[:, None] - pos_k[None, :]).astype(np.float32)
        b = (-slopes * dist + _chunk_valid(pos_q, pos_k, A_BACK)) * np.float32(LOG2E)
        return jnp.asarray(b.reshape(A_KV_HEADS, A_GROUP * tn, pos_k.shape[0]))

    return bias(past_len - la + np.arange(la)), bias(pos_q)


def _sample_bias_b(tables, past_len, lb, tn):
    pos_q = past_len + np.arange(tn)

    def bias(pos_k, offset):
        n_k = pos_k.shape[0]
        ext = _rel_diagonals(tables, tn, offset, n_k + tn - 1)
        rows = [ext[..., tn - 1 - i:tn - 1 - i + n_k] for i in range(tn)]
        valid = _chunk_valid(pos_q, pos_k, B_BACK) * np.float32(LOG2E)
        return jnp.stack(rows, axis=-2) + valid

    return bias(past_len - lb + np.arange(lb), -lb), bias(pos_q, 0)


def _sink_rows(sink, rows):
    s = jnp.repeat(sink.reshape(A_KV_HEADS, A_GROUP, 1) * LOG2E, rows, axis=2)
    return s.reshape(A_KV_HEADS, A_GROUP * rows, 1)


def kernel(x_prompt, x_sample, mem_prompt, cache_a_k, cache_a_v, cache_b_k, cache_b_v, cache_mem_k, cache_mem_v, state_conv, g_mix, w_mix_in, a_sink, b_rel_bias, w_o_a, w_o_b, w_mix_out, g_xattn, g_mem, w_xq, w_xk, w_xv, w_xo, g_ffn, w_up, w_conv, b_conv, w_down, g_final):
    batch, seq, d = x_prompt.shape
    dec_batch, dec_seq, _ = x_sample.shape
    depth = w_mix_in.shape[0]
    n_mem = mem_prompt.shape[1]
    d_ff = w_down.shape[1]
    la, lb = cache_a_k.shape[2], cache_b_k.shape[2]
    past_len = PAST_LEN
    assert batch == 1 and seq % WIDE_TILE == 0 and d_ff % FF_CHUNK == 0
    na, nbk = min(A_BACK * CHUNK, seq), min(B_BACK * CHUNK, seq)
    assert nbk == ROW_TILE and na <= nbk
    ts = dec_batch * dec_seq

    w_in = _prep_w_in(w_mix_in)
    woa = _pair_major(w_o_a, 1).astype(BF16)
    wob = w_o_b.astype(BF16)
    wout = w_mix_out.astype(BF16)
    wxq, wxk, wxv, wxo = (w.astype(BF16) for w in (w_xq, w_xk, w_xv, w_xo))
    wup = w_up.astype(BF16)
    wdown = w_down.astype(BF16)

    rel_tables = b_rel_bias.astype(F32) * LOG2E
    bias_a_p = _prompt_bias_a(ROW_TILE)
    bias_b_p = _prompt_bias_b(rel_tables, ROW_TILE)
    bias_a_sc, bias_a_sn = _sample_bias_a(past_len, la, dec_seq)
    bias_b_sc, bias_b_sn = _sample_bias_b(rel_tables, past_len, lb, dec_seq)
    cak = cache_a_k.reshape(depth, dec_batch, la, A_KV)
    cav = cache_a_v.reshape(depth, dec_batch, la, A_KV)
    cbk = cache_b_k.reshape(depth, dec_batch, lb, B_W)
    cbv = cache_b_v.reshape(depth, dec_batch, lb, B_W)
    cmk = cache_mem_k.reshape(depth, dec_batch, n_mem, X_W)
    cmv = cache_mem_v.reshape(depth, dec_batch, n_mem, X_W)
    zero_state = jnp.zeros((batch, CONV_W - 1, 2 * d_ff), F32)

    pm_k, pm_v = _memkv(mem_prompt[0], g_mem, wxk, wxv)
    pm_k_bf, pm_v_bf = pm_k.astype(BF16), pm_v.astype(BF16)

    xp = x_prompt.reshape(seq, d)
    xs = x_sample.reshape(ts, d)
    outs = {k: [] for k in ("pa_k", "pa_v", "pb_k", "pb_v", "pconv", "sa_k", "sa_v", "sb_k", "sb_v", "sconv")}
    for l in range(depth):
        gfin = g_final if l == depth - 1 else None
        qkv, gate, tail = _inproj(xp, g_mix[l], w_in[l], WIDE_TILE, nbk)
        oa, ob = _attn_prompt(qkv, bias_a_p, bias_b_p, l, _sink_rows(a_sink[l], PAIR))
        xp = _mixout(xp, oa, ob, gate, woa[l], wob[l], wout[l], g_xattn[l], wxq[l], pm_k_bf, pm_v_bf, l,
                     wxo[l], WIDE_TILE)
        xp, cp = _ffn(xp.reshape(batch, seq, d), g_ffn[l], zero_state, wup[l], w_conv[l], b_conv[l],
                      wdown[l], gfin, ROW_TILE)
        xp = xp.reshape(seq, d)
        outs["pb_k"].append(tail[:, 0:B_W].reshape(batch, nbk, B_HEADS, HEAD_DIM))
        outs["pb_v"].append(tail[:, B_W:2 * B_W].reshape(batch, nbk, B_HEADS, HEAD_DIM))
        outs["pa_k"].append(tail[nbk - na:, 2 * B_W:2 * B_W + A_KV].reshape(batch, na, A_KV_HEADS, HEAD_DIM))
        outs["pa_v"].append(tail[nbk - na:, 2 * B_W + A_KV:].reshape(batch, na, A_KV_HEADS, HEAD_DIM))
        outs["pconv"].append(cp)
        qkv, gate, tail = _inproj(xs, g_mix[l], w_in[l], ts, ts)
        oa, ob = _attn_sample(qkv, cak, cav, cbk, cbv, l, dec_seq, bias_a_sc, bias_a_sn,
                              bias_b_sc[l], bias_b_sn[l], _sink_rows(a_sink[l], dec_seq))
        xs = _merge(xs, oa, ob, gate, woa[l], wob[l], wout[l], ts)
        xs = _cross(xs.reshape(dec_batch, dec_seq, d), g_xattn[l], wxq[l], cmk, cmv, wxo[l], dec_seq,
                    lambda b, l=l: (l, b, 0, 0))
        xs, cs = _ffn(xs, g_ffn[l], state_conv[l], wup[l], w_conv[l], b_conv[l], wdown[l], gfin, dec_seq)
        xs = xs.reshape(ts, d)
        outs["sb_k"].append(tail[:, 0:B_W].reshape(dec_batch, dec_seq, B_HEADS, HEAD_DIM))
        outs["sb_v"].append(tail[:, B_W:2 * B_W].reshape(dec_batch, dec_seq, B_HEADS, HEAD_DIM))
        outs["sa_k"].append(tail[:, 2 * B_W:2 * B_W + A_KV].reshape(dec_batch, dec_seq, A_KV_HEADS, HEAD_DIM))
        outs["sa_v"].append(tail[:, 2 * B_W + A_KV:].reshape(dec_batch, dec_seq, A_KV_HEADS, HEAD_DIM))
        outs["sconv"].append(cs)

    st = {k: jnp.stack(v) for k, v in outs.items()}
    pm_k = pm_k.reshape(depth, batch, n_mem, X_HEADS, X_HEAD_DIM)
    pm_v = pm_v.reshape(depth, batch, n_mem, X_HEADS, X_HEAD_DIM)
    return (xp.reshape(batch, seq, d), xs.reshape(dec_batch, dec_seq, d),
            st["pa_k"], st["pa_v"], st["pb_k"], st["pb_v"], pm_k, pm_v, st["pconv"],
            st["sa_k"], st["sa_v"], st["sb_k"], st["sb_v"], st["sconv"])
```

```python
import functools

import numpy as np
import jax
import jax.numpy as jnp
from jax import lax
from jax.experimental import pallas as pl
from jax.experimental.pallas import tpu as pltpu

CHUNK = 64
HEAD_DIM = 64
A_HEADS = 8
A_KV_HEADS = 2
A_GROUP = A_HEADS // A_KV_HEADS
A_BACK = 2
B_HEADS = 8
B_BACK = 8
REL_CLIP = 256
X_HEADS = 4
X_HEAD_DIM = 128
EPS = 1e-6
NEG = -1e30
LOG2E = float(np.log2(np.e))

A_Q = A_HEADS * HEAD_DIM
A_KV = A_KV_HEADS * HEAD_DIM
B_W = B_HEADS * HEAD_DIM
X_W = X_HEADS * X_HEAD_DIM
CONV_W = 3
PAST_LEN = 1024

PAIR = 2 * CHUNK
ROW_TILE = 512
WIDE_TILE = 1024
A_BAND = (A_BACK + 2) * CHUNK
B_BAND = (B_BACK + 2) * CHUNK
FF_CHUNK = 256
FF_DOWN_CUTS = (8,)
QK_AHEAD = 2

LANES = 128
V7X_VMEM_LIMIT = 56 * 1024 * 1024

F32 = jnp.float32
BF16 = jnp.bfloat16


def _params(n_axes=1):
    return pltpu.CompilerParams(
        dimension_semantics=("arbitrary",) * n_axes,
        vmem_limit_bytes=V7X_VMEM_LIMIT)


def _resident(shape, index_map):
    return pl.BlockSpec(shape, index_map, pipeline_mode=pl.Buffered(1))


def _layer_weight(w, layer):
    return pl.BlockSpec((None,) + w.shape[1:], lambda *_: (layer, 0, 0), pipeline_mode=pl.Buffered(1))


def _rmsnorm(x, g):
    ms = jnp.mean(x * x, axis=-1, keepdims=True)
    return (x * lax.rsqrt(ms + EPS)) * g


def _dot(a, b):
    return jnp.dot(a, b, preferred_element_type=F32)


def _dot_nt(a, b):
    return lax.dot_general(a, b, (((1,), (1,)), ((), ())), preferred_element_type=F32)


def _softmax_pv(s_list, v_list, sink=None):
    m = s_list[0].max(axis=-1, keepdims=True)
    for s in s_list[1:]:
        m = jnp.maximum(m, s.max(axis=-1, keepdims=True))
    if sink is not None:
        m = jnp.maximum(m, sink)
    l = None
    o = None
    for s, v in zip(s_list, v_list):
        p = jnp.exp2(s - m)
        ls = p.sum(axis=-1, keepdims=True)
        os_ = _dot(p.astype(BF16), v)
        l = ls if l is None else l + ls
        o = os_ if o is None else o + os_
    if sink is not None:
        l = l + jnp.exp2(sink - m)
    return o * (1.0 / l)


def _lane_lo(shape):
    return lax.broadcasted_iota(jnp.int32, shape, len(shape) - 1) < HEAD_DIM


def _memkv_kernel(mem_ref, g_ref, wk_ref, wv_ref, k_ref, v_ref):
    hm = _rmsnorm(mem_ref[...], g_ref[...]).astype(BF16)
    k_ref[...] = _dot(hm, wk_ref[...])
    v_ref[...] = _dot(hm, wv_ref[...])


def _memkv(mem, g_mem, w_xk, w_xv):
    depth, d, xw = w_xk.shape
    n_mem = mem.shape[0]
    return pl.pallas_call(
        _memkv_kernel,
        grid=(depth,),
        in_specs=[
            pl.BlockSpec((n_mem, d), lambda l: (0, 0)),
            pl.BlockSpec((None, 1, d), lambda l: (l, 0, 0)),
            pl.BlockSpec((None, d, xw), lambda l: (l, 0, 0)),
            pl.BlockSpec((None, d, xw), lambda l: (l, 0, 0)),
        ],
        out_specs=[
            pl.BlockSpec((None, n_mem, xw), lambda l: (l, 0, 0)),
            pl.BlockSpec((None, n_mem, xw), lambda l: (l, 0, 0)),
        ],
        out_shape=[jax.ShapeDtypeStruct((depth, n_mem, xw), F32)] * 2,
        compiler_params=_params(),
        name="mem_kv",
    )(mem, g_mem.reshape(depth, 1, d), w_xk, w_xv)


N_QKV = A_Q + 3 * B_W + 2 * A_KV
N_TAIL = 2 * B_W + 2 * A_KV


def _inproj_kernel(x_ref, g_ref, w_ref, qkv_ref, gate_ref, tail_ref):
    h = _rmsnorm(x_ref[...], g_ref[...]).astype(BF16)
    n_gate = gate_ref.shape[1]
    gate_ref[...] = jax.nn.sigmoid(_dot(h, w_ref[:, :n_gate])).astype(BF16)
    z = _dot(h, w_ref[:, n_gate:])
    qkv_ref[...] = z.astype(BF16)

    @pl.when(pl.program_id(0) == pl.num_programs(0) - 1)
    def _():
        tail_ref[...] = z[z.shape[0] - tail_ref.shape[0]:, N_QKV - N_TAIL:]


def _inproj(x, g, w, layer, tm, tail_rows):
    t, d = x.shape
    n = w.shape[2]
    n_gate = 2 * d
    assert tail_rows <= tm and n == n_gate + N_QKV
    return pl.pallas_call(
        _inproj_kernel,
        grid=(t // tm,),
        in_specs=[
            pl.BlockSpec((tm, d), lambda i: (i, 0)),
            _resident((1, d), lambda i: (0, 0)),
            _layer_weight(w, layer),
        ],
        out_specs=[
            pl.BlockSpec((tm, N_QKV), lambda i: (i, 0)),
            pl.BlockSpec((tm, n_gate), lambda i: (i, 0)),
            pl.BlockSpec((tail_rows, N_TAIL), lambda i: (0, 0)),
        ],
        out_shape=[
            jax.ShapeDtypeStruct((t, N_QKV), BF16),
            jax.ShapeDtypeStruct((t, n_gate), BF16),
            jax.ShapeDtypeStruct((tail_rows, N_TAIL), F32),
        ],
        compiler_params=_params(),
        name="inproj",
    )(x, g.reshape(1, d), w)


def _attn_prompt_kernel(qa_ref, qb_ref, kbc_ref, kbp_ref, vbc_ref, vbp_ref,
                        kvac_ref, kvap_ref, bias_a_ref, bias_b_ref, sink_ref,
                        oa_ref, ob_ref, kcat, vcat, kvacat):
    tm = qa_ref.shape[0]
    npair = tm // PAIR
    first_tile = pl.program_id(0) == 0
    kcat[0:tm, :] = kbp_ref[...]
    kcat[tm:2 * tm, :] = kbc_ref[...]
    vcat[0:tm, :] = vbp_ref[...]
    vcat[tm:2 * tm, :] = vbc_ref[...]
    kvacat[0:PAIR, :] = kvap_ref[...]
    kvacat[PAIR:PAIR + tm, :] = kvac_ref[...]
    lo = _lane_lo((PAIR, 2 * HEAD_DIM))
    lo_a = _lane_lo((A_GROUP * PAIR, 2 * HEAD_DIM))

    def pair_body(d, carry):
        r0 = pl.multiple_of(d * PAIR, PAIR)
        var = jnp.where(first_tile, d, npair)
        ka2 = kvacat[pl.ds(r0, A_BAND), 0:A_KV]
        va2 = kvacat[pl.ds(r0, A_BAND), A_KV:2 * A_KV]
        qa = qa_ref[pl.ds(r0, PAIR), :]
        q_stack = jnp.concatenate(
            [qa[:, p * 2 * HEAD_DIM:(p + 1) * 2 * HEAD_DIM] for p in range(A_GROUP)], axis=0)
        b_operands = {}

        def b_group(hp):
            if hp not in b_operands:
                c0 = hp * 2 * HEAD_DIM
                b_operands[hp] = (qb_ref[pl.ds(r0, PAIR), c0:c0 + 2 * HEAD_DIM],
                                  kcat[pl.ds(r0, B_BAND), c0:c0 + 2 * HEAD_DIM],
                                  vcat[pl.ds(r0, B_BAND), c0:c0 + 2 * HEAD_DIM])
            return b_operands[hp]

        def scores(stage):
            if stage[0] == "a":
                kv = stage[1]
                qm = jnp.where(lo_a, q_stack, 0) if kv == 0 else jnp.where(lo_a, 0, q_stack)
                return _dot_nt(qm, ka2) + bias_a_ref[var, kv]
            _, hp, half = stage
            q2, k2, _ = b_group(hp)
            qm = jnp.where(lo, q2, 0) if half == 0 else jnp.where(lo, 0, q2)
            return _dot_nt(qm, k2) + bias_b_ref[var, 2 * hp + half]

        stages = ([("a", kv) for kv in range(A_KV_HEADS)]
                  + [("b", hp, half) for hp in range(B_HEADS // 2) for half in range(2)])
        done = {}
        pending = [scores(st) for st in stages[:QK_AHEAD]]
        for n, stage in enumerate(stages):
            if n + QK_AHEAD < len(stages):
                pending.append(scores(stages[n + QK_AHEAD]))
            s = pending.pop(0)
            if stage[0] == "a":
                done[stage] = _softmax_pv([s], [va2], sink_ref[stage[1]])
                if stage[1] == A_KV_HEADS - 1:
                    for p in range(A_GROUP):
                        rows = slice(p * PAIR, (p + 1) * PAIR)
                        o_pair = jnp.where(lo, done[("a", 0)][rows], done[("a", 1)][rows])
                        oa_ref[pl.ds(r0, PAIR), p * 2 * HEAD_DIM:(p + 1) * 2 * HEAD_DIM] = o_pair.astype(BF16)
            else:
                _, hp, half = stage
                done[stage] = _softmax_pv([s], [b_group(hp)[2]])
                if half == 1:
                    c0 = hp * 2 * HEAD_DIM
                    ob_ref[pl.ds(r0, PAIR), c0:c0 + 2 * HEAD_DIM] = jnp.where(
                        lo, done[("b", hp, 0)], done[("b", hp, 1)]).astype(BF16)
        return carry

    lax.fori_loop(0, tm // PAIR, pair_body, 0)


def _attn_prompt(qkv, bias_a, bias_b, layer, sink_rows):
    t = qkv.shape[0]
    tm = ROW_TILE
    npair = tm // PAIR
    prev = lambda i: jnp.maximum(i - 1, 0)
    return pl.pallas_call(
        _attn_prompt_kernel,
        grid=(t // tm,),
        in_specs=[
            pl.BlockSpec((tm, A_Q), lambda i: (i, 0)),
            pl.BlockSpec((tm, B_W), lambda i: (i, 1)),
            pl.BlockSpec((tm, B_W), lambda i: (i, 2)),
            pl.BlockSpec((tm, B_W), lambda i: (prev(i), 2)),
            pl.BlockSpec((tm, B_W), lambda i: (i, 3)),
            pl.BlockSpec((tm, B_W), lambda i: (prev(i), 3)),
            pl.BlockSpec((tm, 2 * A_KV), lambda i: (i, (A_Q + 3 * B_W) // (2 * A_KV))),
            pl.BlockSpec((PAIR, 2 * A_KV),
                         lambda i: (jnp.maximum(i * npair - 1, 0), (A_Q + 3 * B_W) // (2 * A_KV))),
            _resident(bias_a.shape, lambda i: (0, 0, 0, 0)),
            _resident((None,) + bias_b.shape[1:], lambda i: (layer, 0, 0, 0, 0)),
            _resident(sink_rows.shape, lambda i: (0, 0, 0)),
        ],
        out_specs=[
            pl.BlockSpec((tm, A_Q), lambda i: (i, 0)),
            pl.BlockSpec((tm, B_W), lambda i: (i, 0)),
        ],
        out_shape=[jax.ShapeDtypeStruct((t, A_Q), BF16), jax.ShapeDtypeStruct((t, B_W), BF16)],
        scratch_shapes=[
            pltpu.VMEM((2 * tm, B_W), BF16),
            pltpu.VMEM((2 * tm, B_W), BF16),
            pltpu.VMEM((PAIR + tm, 2 * A_KV), BF16),
        ],
        compiler_params=_params(),
        name="attn_prompt",
    )(qkv, qkv, qkv, qkv, qkv, qkv, qkv, qkv, bias_a, bias_b, sink_rows)


def _attn_sample_kernel(qa_ref, qb_ref, kbn_ref, vbn_ref, kvan_ref,
                        cak_ref, cav_ref, cbk_ref, cbv_ref,
                        bias_ac_ref, bias_an_ref, bias_bc_ref, bias_bn_ref, sink_ref,
                        oa_ref, ob_ref):
    tn = qa_ref.shape[0]
    lo = _lane_lo((tn, 2 * HEAD_DIM))
    lo_a = _lane_lo((A_GROUP * tn, 2 * HEAD_DIM))
    kac = cak_ref[...].astype(BF16)
    vac = cav_ref[...].astype(BF16)
    kan = kvan_ref[:, 0:A_KV]
    van = kvan_ref[:, A_KV:2 * A_KV]
    qa = qa_ref[...]
    q_stack = jnp.concatenate(
        [qa[:, p * 2 * HEAD_DIM:(p + 1) * 2 * HEAD_DIM] for p in range(A_GROUP)], axis=0)
    b_operands = {}

    def b_group(hp):
        if hp not in b_operands:
            cols = slice(hp * 2 * HEAD_DIM, (hp + 1) * 2 * HEAD_DIM)
            b_operands[hp] = (qb_ref[:, cols], cbk_ref[:, cols].astype(BF16), kbn_ref[:, cols],
                              cbv_ref[:, cols].astype(BF16), vbn_ref[:, cols])
        return b_operands[hp]

    def scores(stage):
        if stage[0] == "a":
            kv = stage[1]
            qm = jnp.where(lo_a, q_stack, 0) if kv == 0 else jnp.where(lo_a, 0, q_stack)
            return [_dot_nt(qm, kac) + bias_ac_ref[kv], _dot_nt(qm, kan) + bias_an_ref[kv]]
        _, hp, half = stage
        q2, kc, kn, _, _ = b_group(hp)
        qm = jnp.where(lo, q2, 0) if half == 0 else jnp.where(lo, 0, q2)
        return [_dot_nt(qm, kc) + bias_bc_ref[2 * hp + half], _dot_nt(qm, kn) + bias_bn_ref[2 * hp + half]]

    stages = ([("a", kv) for kv in range(A_KV_HEADS)]
              + [("b", hp, half) for hp in range(B_HEADS // 2) for half in range(2)])
    done = {}
    pending = [scores(st) for st in stages[:QK_AHEAD]]
    for n, stage in enumerate(stages):
        if n + QK_AHEAD < len(stages):
            pending.append(scores(stages[n + QK_AHEAD]))
        s = pending.pop(0)
        if stage[0] == "a":
            done[stage] = _softmax_pv(s, [vac, van], sink_ref[stage[1]])
            if stage[1] == A_KV_HEADS - 1:
                for p in range(A_GROUP):
                    rows = slice(p * tn, (p + 1) * tn)
                    o_pair = jnp.where(lo, done[("a", 0)][rows], done[("a", 1)][rows])
                    oa_ref[:, p * 2 * HEAD_DIM:(p + 1) * 2 * HEAD_DIM] = o_pair.astype(BF16)
        else:
            _, hp, half = stage
            done[stage] = _softmax_pv(s, list(b_group(hp)[3:5]))
            if half == 1:
                c0 = hp * 2 * HEAD_DIM
                ob_ref[:, c0:c0 + 2 * HEAD_DIM] = jnp.where(
                    lo, done[("b", hp, 0)], done[("b", hp, 1)]).astype(BF16)


def _attn_sample(qkv, cak, cav, cbk, cbv, layer, tn, bias_ac, bias_an, bias_bc, bias_bn, sink_rows):
    t = qkv.shape[0]
    nb = t // tn
    la = cak.shape[2]
    lb = cbk.shape[2]
    kva_blk = (A_Q + 3 * B_W) // (2 * A_KV)
    const3 = lambda b: (0, 0, 0)
    return pl.pallas_call(
        _attn_sample_kernel,
        grid=(nb,),
        in_specs=[
            pl.BlockSpec((tn, A_Q), lambda b: (b, 0)),
            pl.BlockSpec((tn, B_W), lambda b: (b, 1)),
            pl.BlockSpec((tn, B_W), lambda b: (b, 2)),
            pl.BlockSpec((tn, B_W), lambda b: (b, 3)),
            pl.BlockSpec((tn, 2 * A_KV), lambda b: (b, kva_blk)),
            pl.BlockSpec((None, None, la, A_KV), lambda b: (layer, b, 0, 0)),
            pl.BlockSpec((None, None, la, A_KV), lambda b: (layer, b, 0, 0)),
            pl.BlockSpec((None, None, lb, B_W), lambda b: (layer, b, 0, 0)),
            pl.BlockSpec((None, None, lb, B_W), lambda b: (layer, b, 0, 0)),
            _resident(bias_ac.shape, const3),
            _resident(bias_an.shape, const3),
            _resident(bias_bc.shape, const3),
            _resident(bias_bn.shape, const3),
            _resident(sink_rows.shape, const3),
        ],
        out_specs=[
            pl.BlockSpec((tn, A_Q), lambda b: (b, 0)),
            pl.BlockSpec((tn, B_W), lambda b: (b, 0)),
        ],
        out_shape=[jax.ShapeDtypeStruct((t, A_Q), BF16), jax.ShapeDtypeStruct((t, B_W), BF16)],
        compiler_params=_params(),
        name="attn_sample",
    )(qkv, qkv, qkv, qkv, qkv, cak, cav, cbk, cbv, bias_ac, bias_an, bias_bc, bias_bn, sink_rows)


def _merge_rows(x, oa_ref, ob_ref, gate_ref, woa_ref, wob_ref, wout_ref):
    d = x.shape[1]
    a = _dot(oa_ref[...], woa_ref[...])
    b = _dot(ob_ref[...], wob_ref[...])
    m = gate_ref[:, 0:d].astype(F32) * a + gate_ref[:, d:2 * d].astype(F32) * b
    return x + _dot(m.astype(BF16), wout_ref[...])


def _merge_kernel(x_ref, oa_ref, ob_ref, gate_ref, woa_ref, wob_ref, wout_ref, o_ref):
    o_ref[...] = _merge_rows(x_ref[...], oa_ref, ob_ref, gate_ref, woa_ref, wob_ref, wout_ref)


def _merge(x, oa, ob, gate, woa, wob, wout, layer, tm):
    t, d = x.shape
    row = lambda i: (i, 0)
    return pl.pallas_call(
        _merge_kernel,
        grid=(t // tm,),
        in_specs=[
            pl.BlockSpec((tm, d), row),
            pl.BlockSpec((tm, A_Q), row),
            pl.BlockSpec((tm, B_W), row),
            pl.BlockSpec((tm, 2 * d), row),
            _layer_weight(woa, layer),
            _layer_weight(wob, layer),
            _layer_weight(wout, layer),
        ],
        out_specs=pl.BlockSpec((tm, d), row),
        out_shape=jax.ShapeDtypeStruct((t, d), F32),
        compiler_params=_params(),
        name="merge",
    )(x, oa, ob, gate, woa, wob, wout)


def _cross_rows(x, g_ref, wq_ref, mk_ref, mv_ref, wo_ref):
    h = _rmsnorm(x, g_ref[...]).astype(BF16)
    q = _dot(h, wq_ref[...]).astype(BF16)
    mk = mk_ref[...].astype(BF16)
    mv = mv_ref[...].astype(BF16)
    scale = X_HEAD_DIM ** -0.5 * LOG2E

    def scores(hd):
        c0 = hd * X_HEAD_DIM
        return _dot_nt(q[:, c0:c0 + X_HEAD_DIM], mk[:, c0:c0 + X_HEAD_DIM]) * scale

    pending = [scores(hd) for hd in range(min(QK_AHEAD, X_HEADS))]
    outs = []
    for hd in range(X_HEADS):
        if hd + QK_AHEAD < X_HEADS:
            pending.append(scores(hd + QK_AHEAD))
        c0 = hd * X_HEAD_DIM
        outs.append(_softmax_pv([pending.pop(0)], [mv[:, c0:c0 + X_HEAD_DIM]]))
    o = jnp.concatenate(outs, axis=-1).astype(BF16)
    return x + _dot(o, wo_ref[...])


def _cross_kernel(x_ref, g_ref, wq_ref, mk_ref, mv_ref, wo_ref, o_ref):
    o_ref[...] = _cross_rows(x_ref[...], g_ref, wq_ref, mk_ref, mv_ref, wo_ref)


def _mixout_kernel(x_ref, oa_ref, ob_ref, gate_ref, woa_ref, wob_ref, wout_ref,
                   g_ref, wq_ref, mk_ref, mv_ref, wo_ref, o_ref):
    x = _merge_rows(x_ref[...], oa_ref, ob_ref, gate_ref, woa_ref, wob_ref, wout_ref)
    o_ref[...] = _cross_rows(x, g_ref, wq_ref, mk_ref, mv_ref, wo_ref)


def _mixout(x, oa, ob, gate, woa, wob, wout, g, wq, mk, mv, layer, wo, tm):
    t, d = x.shape
    n_mem = mk.shape[-2]
    row = lambda i: (i, 0)
    const = lambda i: (0, 0)
    layer_kv = pl.BlockSpec((None, n_mem, X_W), lambda i: (layer, 0, 0), pipeline_mode=pl.Buffered(1))
    return pl.pallas_call(
        _mixout_kernel,
        grid=(t // tm,),
        in_specs=[
            pl.BlockSpec((tm, d), row),
            pl.BlockSpec((tm, A_Q), row),
            pl.BlockSpec((tm, B_W), row),
            pl.BlockSpec((tm, 2 * d), row),
            _layer_weight(woa, layer),
            _layer_weight(wob, layer),
            _layer_weight(wout, layer),
            _resident((1, d), const),
            _layer_weight(wq, layer),
            layer_kv,
            layer_kv,
            _layer_weight(wo, layer),
        ],
        out_specs=pl.BlockSpec((tm, d), row),
        out_shape=jax.ShapeDtypeStruct((t, d), F32),
        compiler_params=_params(),
        name="mix_out",
    )(x, oa, ob, gate, woa, wob, wout, g.reshape(1, d), wq, mk, mv, wo)


def _cross(x3, g, wq, mk, mv, wo, layer, tm, kv_index):
    nb, rows, d = x3.shape
    n_mem = mk.shape[-2]
    kv_block = (None,) * (mk.ndim - 2) + (n_mem, X_W)
    const = lambda b, i: (0, 0)
    return pl.pallas_call(
        _cross_kernel,
        grid=(nb, rows // tm),
        in_specs=[
            pl.BlockSpec((None, tm, d), lambda b, i: (b, i, 0)),
            _resident((1, d), const),
            _layer_weight(wq, layer),
            pl.BlockSpec(kv_block, lambda b, i: kv_index(b)),
            pl.BlockSpec(kv_block, lambda b, i: kv_index(b)),
            _layer_weight(wo, layer),
        ],
        out_specs=pl.BlockSpec((None, tm, d), lambda b, i: (b, i, 0)),
        out_shape=jax.ShapeDtypeStruct((nb, rows, d), F32),
        compiler_params=_params(2),
        name="cross",
    )(x3, g.reshape(1, d), wq, mk, mv, wo)


def _ffn_kernel(*refs, nb, tm, d_ff, final):
    if final:
        (x_ref, g_ref, state_ref, wup_ref, wconv_ref, bconv_ref, wdown_ref, gfin_ref,
         o_ref, ctail_ref, carry, ebuf, act) = refs
    else:
        (x_ref, g_ref, state_ref, wup_ref, wconv_ref, bconv_ref, wdown_ref,
         o_ref, ctail_ref, carry, ebuf, act) = refs
    i = pl.program_id(0)
    d = x_ref.shape[-1]
    hist = CONV_W - 1
    top = 8

    @pl.when(i == 0)
    def _():
        carry[:, top - hist:top, :] = state_ref[...]

    x = x_ref[...].reshape(nb * tm, d)
    h = _rmsnorm(x, g_ref[...]).astype(BF16)
    n_chunks = d_ff // FF_CHUNK
    for j in range(n_chunks):
        c_half = []
        for half in range(2):
            c0 = half * d_ff + j * FF_CHUNK
            slot = half * n_chunks + j
            u = _dot(h, wup_ref[:, c0:c0 + FF_CHUNK]).reshape(nb, tm, FF_CHUNK)
            ebuf[slot, :, top - hist:top, :] = carry[:, top - hist:top, c0:c0 + FF_CHUNK]
            ebuf[slot, :, top:top + tm, :] = u
            carry[:, top - hist:top, c0:c0 + FF_CHUNK] = u[:, tm - hist:tm, :]
            c = bconv_ref[:, c0:c0 + FF_CHUNK]
            for k in range(hist):
                c = c + ebuf[slot, :, top - hist + k:top - hist + k + tm, :] * wconv_ref[k:k + 1, c0:c0 + FF_CHUNK]
            c = c + u * wconv_ref[hist:hist + 1, c0:c0 + FF_CHUNK]
            c_half.append(c)
        a = jax.nn.silu(c_half[0]) * c_half[1]
        act[:, j * FF_CHUNK:(j + 1) * FF_CHUNK] = a.reshape(nb * tm, FF_CHUNK).astype(BF16)
    cuts = [0, *FF_DOWN_CUTS, n_chunks]
    y = None
    for c_lo, c_hi in zip(cuts[:-1], cuts[1:]):
        rows = slice(c_lo * FF_CHUNK, c_hi * FF_CHUNK)
        part = _dot(act[:, rows], wdown_ref[rows, :])
        y = part if y is None else y + part
    out = x + y
    if final:
        out = _rmsnorm(out, gfin_ref[...])
    o_ref[...] = out.reshape(nb, tm, d)

    @pl.when(i == pl.num_programs(0) - 1)
    def _():
        ctail_ref[...] = carry[:, top - hist:top, :]


def _ffn(x3, g, state, wup, wconv, bconv, wdown, layer, gfin, tm):
    nb, rows, d = x3.shape
    d_ff = wdown.shape[1]
    final = gfin is not None
    const = lambda i: (0, 0)
    in_specs = [
        pl.BlockSpec((nb, tm, d), lambda i: (0, i, 0)),
        _resident((1, d), const),
        _resident(state.shape, lambda i: (0, 0, 0)),
        _layer_weight(wup, layer),
        _resident(wconv.shape, const),
        _resident((1, 2 * d_ff), const),
        _layer_weight(wdown, layer),
    ]
    args = [x3, g.reshape(1, d), state, wup, wconv, bconv.reshape(1, 2 * d_ff), wdown]
    if final:
        in_specs.append(_resident((1, d), const))
        args.append(gfin.reshape(1, d))
    return pl.pallas_call(
        functools.partial(_ffn_kernel, nb=nb, tm=tm, d_ff=d_ff, final=final),
        grid=(rows // tm,),
        in_specs=in_specs,
        out_specs=[
            pl.BlockSpec((nb, tm, d), lambda i: (0, i, 0)),
            pl.BlockSpec((nb, CONV_W - 1, 2 * d_ff), lambda i: (0, 0, 0)),
        ],
        out_shape=[
            jax.ShapeDtypeStruct((nb, rows, d), F32),
            jax.ShapeDtypeStruct((nb, CONV_W - 1, 2 * d_ff), F32),
        ],
        scratch_shapes=[
            pltpu.VMEM((nb, 8, 2 * d_ff), F32),
            pltpu.VMEM((2 * (d_ff // FF_CHUNK), nb, 8 + tm, FF_CHUNK), F32),
            pltpu.VMEM((nb * tm, d_ff), BF16),
        ],
        compiler_params=_params(),
        name="conv_ffn",
    )(*args)


def _alibi_slopes():
    return np.exp2(-8.0 * np.arange(1, A_HEADS + 1) / A_HEADS).astype(np.float32)


def _pair_major(w, axis):
    shp = w.shape
    w = w.reshape(shp[:axis] + (A_KV_HEADS, A_GROUP, HEAD_DIM) + shp[axis + 1:])
    w = jnp.swapaxes(w, axis, axis + 1)
    return w.reshape(shp)


def _prep_w_in(w):
    d = w.shape[1]
    c = np.cumsum([0, A_Q, A_KV, A_KV, B_W, B_W, B_W, d, d])
    qa, ka, va, qb, kb, vb, ga, gb = [w[..., c[i]:c[i + 1]] for i in range(8)]
    qscale = HEAD_DIM ** -0.5 * LOG2E
    qa = _pair_major(qa, 2) * qscale
    return jnp.concatenate([ga, gb, qa, qb * qscale, kb, vb, ka, va], axis=-1).astype(BF16)


def _band_valid(back, band, tm):
    npair = tm // PAIR
    r = np.arange(PAIR)[:, None]
    c = np.arange(band)[None, :]
    qc, kc = r // CHUNK, c // CHUNK
    in_band = (kc >= qc) & (kc <= qc + back)
    history = band - PAIR
    variants = [in_band & (j * PAIR + c >= history) for j in range(npair)] + [in_band]
    return np.where(np.stack(variants), 0.0, NEG).astype(np.float32)


def _prompt_bias_a(tm):
    r = np.arange(PAIR)[:, None]
    c = np.arange(A_BAND)[None, :]
    dist = np.abs(A_BACK * CHUNK + r - c).astype(np.float32)
    slopes = _alibi_slopes().reshape(1, A_KV_HEADS, A_GROUP, 1, 1)
    bias = (-slopes * dist + _band_valid(A_BACK, A_BAND, tm)[:, None, None]) * np.float32(LOG2E)
    return jnp.asarray(bias.reshape(-1, A_KV_HEADS, A_GROUP * PAIR, A_BAND))


def _take_clipped(table, idx):
    idx = np.asarray(idx)
    n_lo = int(np.sum(idx == idx[0])) - 1
    n_hi = int(np.sum(idx == idx[-1])) - 1 if idx[-1] != idx[0] else 0
    want = np.concatenate([np.full(n_lo, idx[0]), np.arange(idx[0], idx[-1] + 1), np.full(n_hi, idx[-1])])
    assert np.array_equal(want, idx)
    lead = table.shape[:-1]
    parts = [jnp.broadcast_to(table[..., idx[0]:idx[0] + 1], lead + (n_lo,)),
             table[..., idx[0]:idx[-1] + 1],
             jnp.broadcast_to(table[..., idx[-1]:idx[-1] + 1], lead + (n_hi,))]
    return jnp.concatenate(parts, axis=-1)


def _rel_diagonals(table, n_q, offset, width):
    idx = np.clip(np.arange(width) - (n_q - 1) + offset, -REL_CLIP, REL_CLIP) + REL_CLIP
    return _take_clipped(table, idx)


def _rel_bias_kernel(ext_ref, valid_ref, out_ref):
    w = ext_ref.shape[-1]
    for h in range(B_HEADS):
        x = jnp.broadcast_to(ext_ref[h:h + 1, :], (PAIR, w))
        y = pltpu.roll(x, w - (PAIR - 1), 1, stride=1, stride_axis=0)[:, :B_BAND]
        for j in range(valid_ref.shape[0]):
            out_ref[j, h] = y + valid_ref[j]


def _prompt_bias_b(tables, tm):
    depth = tables.shape[0]
    w = -(-(PAIR + B_BAND - 1) // LANES) * LANES
    ext = _rel_diagonals(tables, PAIR, -B_BACK * CHUNK, w)
    valid = jnp.asarray(_band_valid(B_BACK, B_BAND, tm) * np.float32(LOG2E))
    nvar = valid.shape[0]
    return pl.pallas_call(
        _rel_bias_kernel,
        grid=(depth,),
        in_specs=[
            pl.BlockSpec((None, B_HEADS, w), lambda l: (l, 0, 0)),
            pl.BlockSpec(valid.shape, lambda l: (0, 0, 0)),
        ],
        out_specs=pl.BlockSpec((None, nvar, B_HEADS, PAIR, B_BAND), lambda l: (l, 0, 0, 0, 0)),
        out_shape=jax.ShapeDtypeStruct((depth, nvar, B_HEADS, PAIR, B_BAND), F32),
        compiler_params=_params(),
        name="rel_bias",
    )(ext, valid)


def _chunk_valid(pos_q, pos_k, back):
    cq, ck = pos_q[:, None] // CHUNK, pos_k[None, :] // CHUNK
    return np.where((ck <= cq) & (ck >= cq - back), 0.0, NEG).astype(np.float32)


def _sample_bias_a(past_len, la, tn):
    pos_q = past_len + np.arange(tn)
    slopes = _alibi_slopes().reshape(A_KV_HEADS, A_GROUP, 1, 1)

    def bias(pos_k):
        dist = np.abs(pos_q[:, None] - pos_k[None, :]).astype(np.float32)
        b = (-slopes * dist + _chunk_valid(pos_q, pos_k, A_BACK)) * np.float32(LOG2E)
        return jnp.asarray(b.reshape(A_KV_HEADS, A_GROUP * tn, pos_k.shape[0]))

    return bias(past_len - la + np.arange(la)), bias(pos_q)


def _sample_bias_b(tables, past_len, lb, tn):
    pos_q = past_len + np.arange(tn)

    def bias(pos_k, offset):
        n_k = pos_k.shape[0]
        ext = _rel_diagonals(tables, tn, offset, n_k + tn - 1)
        rows = [ext[..., tn - 1 - i:tn - 1 - i + n_k] for i in range(tn)]
        valid = _chunk_valid(pos_q, pos_k, B_BACK) * np.float32(LOG2E)
        return jnp.stack(rows, axis=-2) + valid

    return bias(past_len - lb + np.arange(lb), -lb), bias(pos_q, 0)


def _sink_rows(sink, rows):
    s = jnp.repeat(sink.reshape(A_KV_HEADS, A_GROUP, 1) * LOG2E, rows, axis=2)
    return s.reshape(A_KV_HEADS, A_GROUP * rows, 1)


def kernel(x_prompt, x_sample, mem_prompt, cache_a_k, cache_a_v, cache_b_k, cache_b_v, cache_mem_k, cache_mem_v, state_conv, g_mix, w_mix_in, a_sink, b_rel_bias, w_o_a, w_o_b, w_mix_out, g_xattn, g_mem, w_xq, w_xk, w_xv, w_xo, g_ffn, w_up, w_conv, b_conv, w_down, g_final):
    batch, seq, d = x_prompt.shape
    dec_batch, dec_seq, _ = x_sample.shape
    depth = w_mix_in.shape[0]
    n_mem = mem_prompt.shape[1]
    d_ff = w_down.shape[1]
    la, lb = cache_a_k.shape[2], cache_b_k.shape[2]
    past_len = PAST_LEN
    assert batch == 1 and seq % WIDE_TILE == 0 and d_ff % FF_CHUNK == 0
    na, nbk = min(A_BACK * CHUNK, seq), min(B_BACK * CHUNK, seq)
    assert nbk == ROW_TILE and na <= nbk
    ts = dec_batch * dec_seq

    w_in = _prep_w_in(w_mix_in)
    woa = _pair_major(w_o_a, 1).astype(BF16)
    wob = w_o_b.astype(BF16)
    wout = w_mix_out.astype(BF16)
    wxq, wxk, wxv, wxo = (w.astype(BF16) for w in (w_xq, w_xk, w_xv, w_xo))
    wup = w_up.astype(BF16)
    wdown = w_down.astype(BF16)

    rel_tables = b_rel_bias.astype(F32) * LOG2E
    bias_a_p = _prompt_bias_a(ROW_TILE)
    bias_b_p = _prompt_bias_b(rel_tables, ROW_TILE)
    bias_a_sc, bias_a_sn = _sample_bias_a(past_len, la, dec_seq)
    bias_b_sc, bias_b_sn = _sample_bias_b(rel_tables, past_len, lb, dec_seq)
    cak = cache_a_k.reshape(depth, dec_batch, la, A_KV)
    cav = cache_a_v.reshape(depth, dec_batch, la, A_KV)
    cbk = cache_b_k.reshape(depth, dec_batch, lb, B_W)
    cbv = cache_b_v.reshape(depth, dec_batch, lb, B_W)
    cmk = cache_mem_k.reshape(depth, dec_batch, n_mem, X_W)
    cmv = cache_mem_v.reshape(depth, dec_batch, n_mem, X_W)
    zero_state = jnp.zeros((batch, CONV_W - 1, 2 * d_ff), F32)

    pm_k, pm_v = _memkv(mem_prompt[0], g_mem, wxk, wxv)
    pm_k_bf, pm_v_bf = pm_k.astype(BF16), pm_v.astype(BF16)

    xp = x_prompt.reshape(seq, d)
    xs = x_sample.reshape(ts, d)
    outs = {k: [] for k in ("pa_k", "pa_v", "pb_k", "pb_v", "pconv", "sa_k", "sa_v", "sb_k", "sb_v", "sconv")}
    for l in range(depth):
        gfin = g_final if l == depth - 1 else None
        qkv, gate, tail = _inproj(xp, g_mix[l], w_in, l, WIDE_TILE, nbk)
        oa, ob = _attn_prompt(qkv, bias_a_p, bias_b_p, l, _sink_rows(a_sink[l], PAIR))
        xp = _mixout(xp, oa, ob, gate, woa, wob, wout, g_xattn[l], wxq, pm_k_bf, pm_v_bf, l,
                     wxo, WIDE_TILE)
        xp, cp = _ffn(xp.reshape(batch, seq, d), g_ffn[l], zero_state, wup, w_conv[l], b_conv[l],
                      wdown, l, gfin, ROW_TILE)
        xp = xp.reshape(seq, d)
        outs["pb_k"].append(tail[:, 0:B_W].reshape(batch, nbk, B_HEADS, HEAD_DIM))
        outs["pb_v"].append(tail[:, B_W:2 * B_W].reshape(batch, nbk, B_HEADS, HEAD_DIM))
        outs["pa_k"].append(tail[nbk - na:, 2 * B_W:2 * B_W + A_KV].reshape(batch, na, A_KV_HEADS, HEAD_DIM))
        outs["pa_v"].append(tail[nbk - na:, 2 * B_W + A_KV:].reshape(batch, na, A_KV_HEADS, HEAD_DIM))
        outs["pconv"].append(cp)
        qkv, gate, tail = _inproj(xs, g_mix[l], w_in, l, ts, ts)
        oa, ob = _attn_sample(qkv, cak, cav, cbk, cbv, l, dec_seq, bias_a_sc, bias_a_sn,
                              bias_b_sc[l], bias_b_sn[l], _sink_rows(a_sink[l], dec_seq))
        xs = _merge(xs, oa, ob, gate, woa, wob, wout, l, ts)
        xs = _cross(xs.reshape(dec_batch, dec_seq, d), g_xattn[l], wxq, cmk, cmv, wxo, l, dec_seq,
                    lambda b, l=l: (l, b, 0, 0))
        xs, cs = _ffn(xs, g_ffn[l], state_conv[l], wup, w_conv[l], b_conv[l], wdown, l, gfin, dec_seq)
        xs = xs.reshape(ts, d)
        outs["sb_k"].append(tail[:, 0:B_W].reshape(dec_batch, dec_seq, B_HEADS, HEAD_DIM))
        outs["sb_v"].append(tail[:, B_W:2 * B_W].reshape(dec_batch, dec_seq, B_HEADS, HEAD_DIM))
        outs["sa_k"].append(tail[:, 2 * B_W:2 * B_W + A_KV].reshape(dec_batch, dec_seq, A_KV_HEADS, HEAD_DIM))
        outs["sa_v"].append(tail[:, 2 * B_W + A_KV:].reshape(dec_batch, dec_seq, A_KV_HEADS, HEAD_DIM))
        outs["sconv"].append(cs)

    st = {k: jnp.stack(v) for k, v in outs.items()}
    pm_k = pm_k.reshape(depth, batch, n_mem, X_HEADS, X_HEAD_DIM)
    pm_v = pm_v.reshape(depth, batch, n_mem, X_HEADS, X_HEAD_DIM)
    return (xp.reshape(batch, seq, d), xs.reshape(dec_batch, dec_seq, d),
            st["pa_k"], st["pa_v"], st["pb_k"], st["pb_v"], pm_k, pm_v, st["pconv"],
            st["sa_k"], st["sa_v"], st["sb_k"], st["sb_v"], st["sconv"])
```

```python
import functools

import numpy as np
import jax
import jax.numpy as jnp
from jax import lax
from jax.experimental import pallas as pl
from jax.experimental.pallas import tpu as pltpu

CHUNK = 64
HEAD_DIM = 64
A_HEADS = 8
A_KV_HEADS = 2
A_GROUP = A_HEADS // A_KV_HEADS
A_BACK = 2
B_HEADS = 8
B_BACK = 8
REL_CLIP = 256
X_HEADS = 4
X_HEAD_DIM = 128
EPS = 1e-6
NEG = -1e30
LOG2E = float(np.log2(np.e))

A_Q = A_HEADS * HEAD_DIM
A_KV = A_KV_HEADS * HEAD_DIM
B_W = B_HEADS * HEAD_DIM
X_W = X_HEADS * X_HEAD_DIM
CONV_W = 3
PAST_LEN = 1024

PAIR = 2 * CHUNK
ROW_TILE = 512
WIDE_TILE = 1024
A_BAND = (A_BACK + 2) * CHUNK
B_BAND = (B_BACK + 2) * CHUNK
FF_CHUNK = 256
FF_DOWN_CUTS = (8,)
QK_AHEAD = 2

LANES = 128
V7X_VMEM_LIMIT = 56 * 1024 * 1024

F32 = jnp.float32
BF16 = jnp.bfloat16


def _params(n_axes=1):
    return pltpu.CompilerParams(
        dimension_semantics=("arbitrary",) * n_axes,
        vmem_limit_bytes=V7X_VMEM_LIMIT)


def _resident(shape, index_map):
    return pl.BlockSpec(shape, index_map, pipeline_mode=pl.Buffered(1))


def _layer_weight(w, layer):
    return pl.BlockSpec((None,) + w.shape[1:], lambda *_: (layer, 0, 0), pipeline_mode=pl.Buffered(1))


def _rmsnorm(x, g):
    ms = jnp.mean(x * x, axis=-1, keepdims=True)
    return (x * lax.rsqrt(ms + EPS)) * g


def _dot(a, b):
    return jnp.dot(a, b, preferred_element_type=F32)


def _dot_nt(a, b):
    return lax.dot_general(a, b, (((1,), (1,)), ((), ())), preferred_element_type=F32)


def _softmax_pv(s_list, v_list, sink=None):
    m = s_list[0].max(axis=-1, keepdims=True)
    for s in s_list[1:]:
        m = jnp.maximum(m, s.max(axis=-1, keepdims=True))
    if sink is not None:
        m = jnp.maximum(m, sink)
    l = None
    o = None
    for s, v in zip(s_list, v_list):
        p = jnp.exp2(s - m)
        ls = p.sum(axis=-1, keepdims=True)
        os_ = _dot(p.astype(BF16), v)
        l = ls if l is None else l + ls
        o = os_ if o is None else o + os_
    if sink is not None:
        l = l + jnp.exp2(sink - m)
    return o * (1.0 / l)


def _softmax_pv_lanesum(s, v_aug):
    m = s.max(axis=-1, keepdims=True)
    o = _dot(jnp.exp2(s - m).astype(BF16), v_aug)
    return o * (1.0 / pltpu.roll(o, HEAD_DIM, 1))


def _lane_lo(shape):
    return lax.broadcasted_iota(jnp.int32, shape, len(shape) - 1) < HEAD_DIM


def _memkv_kernel(mem_ref, g_ref, wk_ref, wv_ref, k_ref, v_ref):
    hm = _rmsnorm(mem_ref[...], g_ref[...]).astype(BF16)
    k_ref[...] = _dot(hm, wk_ref[...])
    v_ref[...] = _dot(hm, wv_ref[...])


def _memkv(mem, g_mem, w_xk, w_xv):
    depth, d, xw = w_xk.shape
    n_mem = mem.shape[0]
    return pl.pallas_call(
        _memkv_kernel,
        grid=(depth,),
        in_specs=[
            pl.BlockSpec((n_mem, d), lambda l: (0, 0)),
            pl.BlockSpec((None, 1, d), lambda l: (l, 0, 0)),
            pl.BlockSpec((None, d, xw), lambda l: (l, 0, 0)),
            pl.BlockSpec((None, d, xw), lambda l: (l, 0, 0)),
        ],
        out_specs=[
            pl.BlockSpec((None, n_mem, xw), lambda l: (l, 0, 0)),
            pl.BlockSpec((None, n_mem, xw), lambda l: (l, 0, 0)),
        ],
        out_shape=[jax.ShapeDtypeStruct((depth, n_mem, xw), F32)] * 2,
        compiler_params=_params(),
        name="mem_kv",
    )(mem, g_mem.reshape(depth, 1, d), w_xk, w_xv)


N_QKV = A_Q + 3 * B_W + 2 * A_KV
N_TAIL = 2 * B_W + 2 * A_KV


def _inproj_kernel(x_ref, g_ref, w_ref, qkv_ref, gate_ref, tail_ref):
    h = _rmsnorm(x_ref[...], g_ref[...]).astype(BF16)
    n_gate = gate_ref.shape[1]
    gate_ref[...] = jax.nn.sigmoid(_dot(h, w_ref[:, :n_gate])).astype(BF16)
    z = _dot(h, w_ref[:, n_gate:])
    qkv_ref[...] = z.astype(BF16)

    @pl.when(pl.program_id(0) == pl.num_programs(0) - 1)
    def _():
        tail_ref[...] = z[z.shape[0] - tail_ref.shape[0]:, N_QKV - N_TAIL:]


def _inproj(x, g, w, layer, tm, tail_rows):
    t, d = x.shape
    n = w.shape[2]
    n_gate = 2 * d
    assert tail_rows <= tm and n == n_gate + N_QKV
    return pl.pallas_call(
        _inproj_kernel,
        grid=(t // tm,),
        in_specs=[
            pl.BlockSpec((tm, d), lambda i: (i, 0)),
            _resident((1, d), lambda i: (0, 0)),
            _layer_weight(w, layer),
        ],
        out_specs=[
            pl.BlockSpec((tm, N_QKV), lambda i: (i, 0)),
            pl.BlockSpec((tm, n_gate), lambda i: (i, 0)),
            pl.BlockSpec((tail_rows, N_TAIL), lambda i: (0, 0)),
        ],
        out_shape=[
            jax.ShapeDtypeStruct((t, N_QKV), BF16),
            jax.ShapeDtypeStruct((t, n_gate), BF16),
            jax.ShapeDtypeStruct((tail_rows, N_TAIL), F32),
        ],
        compiler_params=_params(),
        name="inproj",
    )(x, g.reshape(1, d), w)


def _attn_prompt_kernel(qa_ref, qb_ref, kbc_ref, kbp_ref, vbc_ref, vbp_ref,
                        kvac_ref, kvap_ref, bias_a_ref, bias_b_ref, sink_ref,
                        oa_ref, ob_ref, kcat, vcat, kvacat):
    tm = qa_ref.shape[0]
    npair = tm // PAIR
    first_tile = pl.program_id(0) == 0
    kcat[0:tm, :] = kbp_ref[...]
    kcat[tm:2 * tm, :] = kbc_ref[...]
    vcat[0:tm, :] = vbp_ref[...]
    vcat[tm:2 * tm, :] = vbc_ref[...]
    kvacat[0:PAIR, :] = kvap_ref[...]
    kvacat[PAIR:PAIR + tm, :] = kvac_ref[...]
    lo = _lane_lo((PAIR, 2 * HEAD_DIM))
    lo_a = _lane_lo((A_GROUP * PAIR, 2 * HEAD_DIM))
    lo_v = _lane_lo((B_BAND, 2 * HEAD_DIM))

    def pair_body(d, carry):
        r0 = pl.multiple_of(d * PAIR, PAIR)
        var = jnp.where(first_tile, d, npair)
        ka2 = kvacat[pl.ds(r0, A_BAND), 0:A_KV]
        va2 = kvacat[pl.ds(r0, A_BAND), A_KV:2 * A_KV]
        qa = qa_ref[pl.ds(r0, PAIR), :]
        q_stack = jnp.concatenate(
            [qa[:, p * 2 * HEAD_DIM:(p + 1) * 2 * HEAD_DIM] for p in range(A_GROUP)], axis=0)
        b_operands = {}

        def b_group(hp):
            if hp not in b_operands:
                c0 = hp * 2 * HEAD_DIM
                b_operands[hp] = (qb_ref[pl.ds(r0, PAIR), c0:c0 + 2 * HEAD_DIM],
                                  kcat[pl.ds(r0, B_BAND), c0:c0 + 2 * HEAD_DIM],
                                  vcat[pl.ds(r0, B_BAND), c0:c0 + 2 * HEAD_DIM])
            return b_operands[hp]

        def scores(stage):
            if stage[0] == "a":
                kv = stage[1]
                qm = jnp.where(lo_a, q_stack, 0) if kv == 0 else jnp.where(lo_a, 0, q_stack)
                return _dot_nt(qm, ka2) + bias_a_ref[var, kv]
            _, hp, half = stage
            q2, k2, _ = b_group(hp)
            qm = jnp.where(lo, q2, 0) if half == 0 else jnp.where(lo, 0, q2)
            return _dot_nt(qm, k2) + bias_b_ref[var, 2 * hp + half]

        stages = ([("a", kv) for kv in range(A_KV_HEADS)]
                  + [("b", hp, half) for hp in range(B_HEADS // 2) for half in range(2)])
        done = {}
        pending = [scores(st) for st in stages[:QK_AHEAD]]
        for n, stage in enumerate(stages):
            if n + QK_AHEAD < len(stages):
                pending.append(scores(stages[n + QK_AHEAD]))
            s = pending.pop(0)
            if stage[0] == "a":
                done[stage] = _softmax_pv([s], [va2], sink_ref[stage[1]])
                if stage[1] == A_KV_HEADS - 1:
                    for p in range(A_GROUP):
                        rows = slice(p * PAIR, (p + 1) * PAIR)
                        o_pair = jnp.where(lo, done[("a", 0)][rows], done[("a", 1)][rows])
                        oa_ref[pl.ds(r0, PAIR), p * 2 * HEAD_DIM:(p + 1) * 2 * HEAD_DIM] = o_pair.astype(BF16)
            else:
                _, hp, half = stage
                v2 = b_group(hp)[2]
                v_aug = jnp.where(lo_v, v2, 1) if half == 0 else jnp.where(lo_v, 1, v2)
                done[stage] = _softmax_pv_lanesum(s, v_aug)
                if half == 1:
                    c0 = hp * 2 * HEAD_DIM
                    ob_ref[pl.ds(r0, PAIR), c0:c0 + 2 * HEAD_DIM] = jnp.where(
                        lo, done[("b", hp, 0)], done[("b", hp, 1)]).astype(BF16)
        return carry

    lax.fori_loop(0, tm // PAIR, pair_body, 0)


def _attn_prompt(qkv, bias_a, bias_b, layer, sink_rows):
    t = qkv.shape[0]
    tm = ROW_TILE
    npair = tm // PAIR
    prev = lambda i: jnp.maximum(i - 1, 0)
    return pl.pallas_call(
        _attn_prompt_kernel,
        grid=(t // tm,),
        in_specs=[
            pl.BlockSpec((tm, A_Q), lambda i: (i, 0)),
            pl.BlockSpec((tm, B_W), lambda i: (i, 1)),
            pl.BlockSpec((tm, B_W), lambda i: (i, 2)),
            pl.BlockSpec((tm, B_W), lambda i: (prev(i), 2)),
            pl.BlockSpec((tm, B_W), lambda i: (i, 3)),
            pl.BlockSpec((tm, B_W), lambda i: (prev(i), 3)),
            pl.BlockSpec((tm, 2 * A_KV), lambda i: (i, (A_Q + 3 * B_W) // (2 * A_KV))),
            pl.BlockSpec((PAIR, 2 * A_KV),
                         lambda i: (jnp.maximum(i * npair - 1, 0), (A_Q + 3 * B_W) // (2 * A_KV))),
            _resident(bias_a.shape, lambda i: (0, 0, 0, 0)),
            _resident((None,) + bias_b.shape[1:], lambda i: (layer, 0, 0, 0, 0)),
            _resident(sink_rows.shape, lambda i: (0, 0, 0)),
        ],
        out_specs=[
            pl.BlockSpec((tm, A_Q), lambda i: (i, 0)),
            pl.BlockSpec((tm, B_W), lambda i: (i, 0)),
        ],
        out_shape=[jax.ShapeDtypeStruct((t, A_Q), BF16), jax.ShapeDtypeStruct((t, B_W), BF16)],
        scratch_shapes=[
            pltpu.VMEM((2 * tm, B_W), BF16),
            pltpu.VMEM((2 * tm, B_W), BF16),
            pltpu.VMEM((PAIR + tm, 2 * A_KV), BF16),
        ],
        compiler_params=_params(),
        name="attn_prompt",
    )(qkv, qkv, qkv, qkv, qkv, qkv, qkv, qkv, bias_a, bias_b, sink_rows)


def _attn_sample_kernel(qa_ref, qb_ref, kbn_ref, vbn_ref, kvan_ref,
                        cak_ref, cav_ref, cbk_ref, cbv_ref,
                        bias_ac_ref, bias_an_ref, bias_bc_ref, bias_bn_ref, sink_ref,
                        oa_ref, ob_ref):
    tn = qa_ref.shape[0]
    lo = _lane_lo((tn, 2 * HEAD_DIM))
    lo_a = _lane_lo((A_GROUP * tn, 2 * HEAD_DIM))
    kac = cak_ref[...].astype(BF16)
    vac = cav_ref[...].astype(BF16)
    kan = kvan_ref[:, 0:A_KV]
    van = kvan_ref[:, A_KV:2 * A_KV]
    qa = qa_ref[...]
    q_stack = jnp.concatenate(
        [qa[:, p * 2 * HEAD_DIM:(p + 1) * 2 * HEAD_DIM] for p in range(A_GROUP)], axis=0)
    b_operands = {}

    def b_group(hp):
        if hp not in b_operands:
            cols = slice(hp * 2 * HEAD_DIM, (hp + 1) * 2 * HEAD_DIM)
            b_operands[hp] = (qb_ref[:, cols], cbk_ref[:, cols].astype(BF16), kbn_ref[:, cols],
                              cbv_ref[:, cols].astype(BF16), vbn_ref[:, cols])
        return b_operands[hp]

    def scores(stage):
        if stage[0] == "a":
            kv = stage[1]
            qm = jnp.where(lo_a, q_stack, 0) if kv == 0 else jnp.where(lo_a, 0, q_stack)
            return [_dot_nt(qm, kac) + bias_ac_ref[kv], _dot_nt(qm, kan) + bias_an_ref[kv]]
        _, hp, half = stage
        q2, kc, kn, _, _ = b_group(hp)
        qm = jnp.where(lo, q2, 0) if half == 0 else jnp.where(lo, 0, q2)
        return [_dot_nt(qm, kc) + bias_bc_ref[2 * hp + half], _dot_nt(qm, kn) + bias_bn_ref[2 * hp + half]]

    stages = ([("a", kv) for kv in range(A_KV_HEADS)]
              + [("b", hp, half) for hp in range(B_HEADS // 2) for half in range(2)])
    done = {}
    pending = [scores(st) for st in stages[:QK_AHEAD]]
    for n, stage in enumerate(stages):
        if n + QK_AHEAD < len(stages):
            pending.append(scores(stages[n + QK_AHEAD]))
        s = pending.pop(0)
        if stage[0] == "a":
            done[stage] = _softmax_pv(s, [vac, van], sink_ref[stage[1]])
            if stage[1] == A_KV_HEADS - 1:
                for p in range(A_GROUP):
                    rows = slice(p * tn, (p + 1) * tn)
                    o_pair = jnp.where(lo, done[("a", 0)][rows], done[("a", 1)][rows])
                    oa_ref[:, p * 2 * HEAD_DIM:(p + 1) * 2 * HEAD_DIM] = o_pair.astype(BF16)
        else:
            _, hp, half = stage
            done[stage] = _softmax_pv(s, list(b_group(hp)[3:5]))
            if half == 1:
                c0 = hp * 2 * HEAD_DIM
                ob_ref[:, c0:c0 + 2 * HEAD_DIM] = jnp.where(
                    lo, done[("b", hp, 0)], done[("b", hp, 1)]).astype(BF16)


def _attn_sample(qkv, cak, cav, cbk, cbv, layer, tn, bias_ac, bias_an, bias_bc, bias_bn, sink_rows):
    t = qkv.shape[0]
    nb = t // tn
    la = cak.shape[2]
    lb = cbk.shape[2]
    kva_blk = (A_Q + 3 * B_W) // (2 * A_KV)
    const3 = lambda b: (0, 0, 0)
    return pl.pallas_call(
        _attn_sample_kernel,
        grid=(nb,),
        in_specs=[
            pl.BlockSpec((tn, A_Q), lambda b: (b, 0)),
            pl.BlockSpec((tn, B_W), lambda b: (b, 1)),
            pl.BlockSpec((tn, B_W), lambda b: (b, 2)),
            pl.BlockSpec((tn, B_W), lambda b: (b, 3)),
            pl.BlockSpec((tn, 2 * A_KV), lambda b: (b, kva_blk)),
            pl.BlockSpec((None, None, la, A_KV), lambda b: (layer, b, 0, 0)),
            pl.BlockSpec((None, None, la, A_KV), lambda b: (layer, b, 0, 0)),
            pl.BlockSpec((None, None, lb, B_W), lambda b: (layer, b, 0, 0)),
            pl.BlockSpec((None, None, lb, B_W), lambda b: (layer, b, 0, 0)),
            _resident(bias_ac.shape, const3),
            _resident(bias_an.shape, const3),
            _resident(bias_bc.shape, const3),
            _resident(bias_bn.shape, const3),
            _resident(sink_rows.shape, const3),
        ],
        out_specs=[
            pl.BlockSpec((tn, A_Q), lambda b: (b, 0)),
            pl.BlockSpec((tn, B_W), lambda b: (b, 0)),
        ],
        out_shape=[jax.ShapeDtypeStruct((t, A_Q), BF16), jax.ShapeDtypeStruct((t, B_W), BF16)],
        compiler_params=_params(),
        name="attn_sample",
    )(qkv, qkv, qkv, qkv, qkv, cak, cav, cbk, cbv, bias_ac, bias_an, bias_bc, bias_bn, sink_rows)


def _merge_rows(x, oa_ref, ob_ref, gate_ref, woa_ref, wob_ref, wout_ref):
    d = x.shape[1]
    a = _dot(oa_ref[...], woa_ref[...])
    b = _dot(ob_ref[...], wob_ref[...])
    m = gate_ref[:, 0:d].astype(F32) * a + gate_ref[:, d:2 * d].astype(F32) * b
    return x + _dot(m.astype(BF16), wout_ref[...])


def _merge_kernel(x_ref, oa_ref, ob_ref, gate_ref, woa_ref, wob_ref, wout_ref, o_ref):
    o_ref[...] = _merge_rows(x_ref[...], oa_ref, ob_ref, gate_ref, woa_ref, wob_ref, wout_ref)


def _merge(x, oa, ob, gate, woa, wob, wout, layer, tm):
    t, d = x.shape
    row = lambda i: (i, 0)
    return pl.pallas_call(
        _merge_kernel,
        grid=(t // tm,),
        in_specs=[
            pl.BlockSpec((tm, d), row),
            pl.BlockSpec((tm, A_Q), row),
            pl.BlockSpec((tm, B_W), row),
            pl.BlockSpec((tm, 2 * d), row),
            _layer_weight(woa, layer),
            _layer_weight(wob, layer),
            _layer_weight(wout, layer),
        ],
        out_specs=pl.BlockSpec((tm, d), row),
        out_shape=jax.ShapeDtypeStruct((t, d), F32),
        compiler_params=_params(),
        name="merge",
    )(x, oa, ob, gate, woa, wob, wout)


def _cross_rows(x, g_ref, wq_ref, mk_ref, mv_ref, wo_ref):
    h = _rmsnorm(x, g_ref[...]).astype(BF16)
    q = _dot(h, wq_ref[...]).astype(BF16)
    mk = mk_ref[...].astype(BF16)
    mv = mv_ref[...].astype(BF16)
    scale = X_HEAD_DIM ** -0.5 * LOG2E

    def scores(hd):
        c0 = hd * X_HEAD_DIM
        return _dot_nt(q[:, c0:c0 + X_HEAD_DIM], mk[:, c0:c0 + X_HEAD_DIM]) * scale

    pending = [scores(hd) for hd in range(min(QK_AHEAD, X_HEADS))]
    outs = []
    for hd in range(X_HEADS):
        if hd + QK_AHEAD < X_HEADS:
            pending.append(scores(hd + QK_AHEAD))
        c0 = hd * X_HEAD_DIM
        outs.append(_softmax_pv([pending.pop(0)], [mv[:, c0:c0 + X_HEAD_DIM]]))
    o = jnp.concatenate(outs, axis=-1).astype(BF16)
    return x + _dot(o, wo_ref[...])


def _cross_kernel(x_ref, g_ref, wq_ref, mk_ref, mv_ref, wo_ref, o_ref):
    o_ref[...] = _cross_rows(x_ref[...], g_ref, wq_ref, mk_ref, mv_ref, wo_ref)


def _mixout_kernel(x_ref, oa_ref, ob_ref, gate_ref, woa_ref, wob_ref, wout_ref,
                   g_ref, wq_ref, mk_ref, mv_ref, wo_ref, o_ref):
    x = _merge_rows(x_ref[...], oa_ref, ob_ref, gate_ref, woa_ref, wob_ref, wout_ref)
    o_ref[...] = _cross_rows(x, g_ref, wq_ref, mk_ref, mv_ref, wo_ref)


def _mixout(x, oa, ob, gate, woa, wob, wout, g, wq, mk, mv, layer, wo, tm):
    t, d = x.shape
    n_mem = mk.shape[-2]
    row = lambda i: (i, 0)
    const = lambda i: (0, 0)
    layer_kv = pl.BlockSpec((None, n_mem, X_W), lambda i: (layer, 0, 0), pipeline_mode=pl.Buffered(1))
    return pl.pallas_call(
        _mixout_kernel,
        grid=(t // tm,),
        in_specs=[
            pl.BlockSpec((tm, d), row),
            pl.BlockSpec((tm, A_Q), row),
            pl.BlockSpec((tm, B_W), row),
            pl.BlockSpec((tm, 2 * d), row),
            _layer_weight(woa, layer),
            _layer_weight(wob, layer),
            _layer_weight(wout, layer),
            _resident((1, d), const),
            _layer_weight(wq, layer),
            layer_kv,
            layer_kv,
            _layer_weight(wo, layer),
        ],
        out_specs=pl.BlockSpec((tm, d), row),
        out_shape=jax.ShapeDtypeStruct((t, d), F32),
        compiler_params=_params(),
        name="mix_out",
    )(x, oa, ob, gate, woa, wob, wout, g.reshape(1, d), wq, mk, mv, wo)


def _cross(x3, g, wq, mk, mv, wo, layer, tm, kv_index):
    nb, rows, d = x3.shape
    n_mem = mk.shape[-2]
    kv_block = (None,) * (mk.ndim - 2) + (n_mem, X_W)
    const = lambda b, i: (0, 0)
    return pl.pallas_call(
        _cross_kernel,
        grid=(nb, rows // tm),
        in_specs=[
            pl.BlockSpec((None, tm, d), lambda b, i: (b, i, 0)),
            _resident((1, d), const),
            _layer_weight(wq, layer),
            pl.BlockSpec(kv_block, lambda b, i: kv_index(b)),
            pl.BlockSpec(kv_block, lambda b, i: kv_index(b)),
            _layer_weight(wo, layer),
        ],
        out_specs=pl.BlockSpec((None, tm, d), lambda b, i: (b, i, 0)),
        out_shape=jax.ShapeDtypeStruct((nb, rows, d), F32),
        compiler_params=_params(2),
        name="cross",
    )(x3, g.reshape(1, d), wq, mk, mv, wo)


def _ffn_kernel(*refs, nb, tm, d_ff, final):
    if final:
        (x_ref, g_ref, state_ref, wup_ref, wconv_ref, bconv_ref, wdown_ref, gfin_ref,
         o_ref, ctail_ref, carry, ebuf, act) = refs
    else:
        (x_ref, g_ref, state_ref, wup_ref, wconv_ref, bconv_ref, wdown_ref,
         o_ref, ctail_ref, carry, ebuf, act) = refs
    i = pl.program_id(0)
    d = x_ref.shape[-1]
    hist = CONV_W - 1
    top = 8

    @pl.when(i == 0)
    def _():
        carry[:, top - hist:top, :] = state_ref[...]

    x = x_ref[...].reshape(nb * tm, d)
    h = _rmsnorm(x, g_ref[...]).astype(BF16)
    n_chunks = d_ff // FF_CHUNK
    for j in range(n_chunks):
        c_half = []
        for half in range(2):
            c0 = half * d_ff + j * FF_CHUNK
            slot = half * n_chunks + j
            u = _dot(h, wup_ref[:, c0:c0 + FF_CHUNK]).reshape(nb, tm, FF_CHUNK)
            ebuf[slot, :, top - hist:top, :] = carry[:, top - hist:top, c0:c0 + FF_CHUNK]
            ebuf[slot, :, top:top + tm, :] = u
            carry[:, top - hist:top, c0:c0 + FF_CHUNK] = u[:, tm - hist:tm, :]
            c = bconv_ref[:, c0:c0 + FF_CHUNK]
            for k in range(hist):
                c = c + ebuf[slot, :, top - hist + k:top - hist + k + tm, :] * wconv_ref[k:k + 1, c0:c0 + FF_CHUNK]
            c = c + u * wconv_ref[hist:hist + 1, c0:c0 + FF_CHUNK]
            c_half.append(c)
        a = jax.nn.silu(c_half[0]) * c_half[1]
        act[:, j * FF_CHUNK:(j + 1) * FF_CHUNK] = a.reshape(nb * tm, FF_CHUNK).astype(BF16)
    cuts = [0, *FF_DOWN_CUTS, n_chunks]
    y = None
    for c_lo, c_hi in zip(cuts[:-1], cuts[1:]):
        rows = slice(c_lo * FF_CHUNK, c_hi * FF_CHUNK)
        part = _dot(act[:, rows], wdown_ref[rows, :])
        y = part if y is None else y + part
    out = x + y
    if final:
        out = _rmsnorm(out, gfin_ref[...])
    o_ref[...] = out.reshape(nb, tm, d)

    @pl.when(i == pl.num_programs(0) - 1)
    def _():
        ctail_ref[...] = carry[:, top - hist:top, :]


def _ffn(x3, g, state, wup, wconv, bconv, wdown, layer, gfin, tm):
    nb, rows, d = x3.shape
    d_ff = wdown.shape[1]
    final = gfin is not None
    const = lambda i: (0, 0)
    in_specs = [
        pl.BlockSpec((nb, tm, d), lambda i: (0, i, 0)),
        _resident((1, d), const),
        _resident(state.shape, lambda i: (0, 0, 0)),
        _layer_weight(wup, layer),
        _resident(wconv.shape, const),
        _resident((1, 2 * d_ff), const),
        _layer_weight(wdown, layer),
    ]
    args = [x3, g.reshape(1, d), state, wup, wconv, bconv.reshape(1, 2 * d_ff), wdown]
    if final:
        in_specs.append(_resident((1, d), const))
        args.append(gfin.reshape(1, d))
    return pl.pallas_call(
        functools.partial(_ffn_kernel, nb=nb, tm=tm, d_ff=d_ff, final=final),
        grid=(rows // tm,),
        in_specs=in_specs,
        out_specs=[
            pl.BlockSpec((nb, tm, d), lambda i: (0, i, 0)),
            pl.BlockSpec((nb, CONV_W - 1, 2 * d_ff), lambda i: (0, 0, 0)),
        ],
        out_shape=[
            jax.ShapeDtypeStruct((nb, rows, d), F32),
            jax.ShapeDtypeStruct((nb, CONV_W - 1, 2 * d_ff), F32),
        ],
        scratch_shapes=[
            pltpu.VMEM((nb, 8, 2 * d_ff), F32),
            pltpu.VMEM((2 * (d_ff // FF_CHUNK), nb, 8 + tm, FF_CHUNK), F32),
            pltpu.VMEM((nb * tm, d_ff), BF16),
        ],
        compiler_params=_params(),
        name="conv_ffn",
    )(*args)


def _alibi_slopes():
    return np.exp2(-8.0 * np.arange(1, A_HEADS + 1) / A_HEADS).astype(np.float32)


def _pair_major(w, axis):
    shp = w.shape
    w = w.reshape(shp[:axis] + (A_KV_HEADS, A_GROUP, HEAD_DIM) + shp[axis + 1:])
    w = jnp.swapaxes(w, axis, axis + 1)
    return w.reshape(shp)


def _prep_w_in(w):
    d = w.shape[1]
    c = np.cumsum([0, A_Q, A_KV, A_KV, B_W, B_W, B_W, d, d])
    qa, ka, va, qb, kb, vb, ga, gb = [w[..., c[i]:c[i + 1]] for i in range(8)]
    qscale = HEAD_DIM ** -0.5 * LOG2E
    qa = _pair_major(qa, 2) * qscale
    return jnp.concatenate([ga, gb, qa, qb * qscale, kb, vb, ka, va], axis=-1).astype(BF16)


def _band_valid(back, band, tm):
    npair = tm // PAIR
    r = np.arange(PAIR)[:, None]
    c = np.arange(band)[None, :]
    qc, kc = r // CHUNK, c // CHUNK
    in_band = (kc >= qc) & (kc <= qc + back)
    history = band - PAIR
    variants = [in_band & (j * PAIR + c >= history) for j in range(npair)] + [in_band]
    return np.where(np.stack(variants), 0.0, NEG).astype(np.float32)


def _prompt_bias_a(tm):
    r = np.arange(PAIR)[:, None]
    c = np.arange(A_BAND)[None, :]
    dist = np.abs(A_BACK * CHUNK + r - c).astype(np.float32)
    slopes = _alibi_slopes().reshape(1, A_KV_HEADS, A_GROUP, 1, 1)
    bias = (-slopes * dist + _band_valid(A_BACK, A_BAND, tm)[:, None, None]) * np.float32(LOG2E)
    return jnp.asarray(bias.reshape(-1, A_KV_HEADS, A_GROUP * PAIR, A_BAND))


def _take_clipped(table, idx):
    idx = np.asarray(idx)
    n_lo = int(np.sum(idx == idx[0])) - 1
    n_hi = int(np.sum(idx == idx[-1])) - 1 if idx[-1] != idx[0] else 0
    want = np.concatenate([np.full(n_lo, idx[0]), np.arange(idx[0], idx[-1] + 1), np.full(n_hi, idx[-1])])
    assert np.array_equal(want, idx)
    lead = table.shape[:-1]
    parts = [jnp.broadcast_to(table[..., idx[0]:idx[0] + 1], lead + (n_lo,)),
             table[..., idx[0]:idx[-1] + 1],
             jnp.broadcast_to(table[..., idx[-1]:idx[-1] + 1], lead + (n_hi,))]
    return jnp.concatenate(parts, axis=-1)


def _rel_diagonals(table, n_q, offset, width):
    idx = np.clip(np.arange(width) - (n_q - 1) + offset, -REL_CLIP, REL_CLIP) + REL_CLIP
    return _take_clipped(table, idx)


def _rel_bias_kernel(ext_ref, valid_ref, out_ref):
    w = ext_ref.shape[-1]
    for h in range(B_HEADS):
        x = jnp.broadcast_to(ext_ref[h:h + 1, :], (PAIR, w))
        y = pltpu.roll(x, w - (PAIR - 1), 1, stride=1, stride_axis=0)[:, :B_BAND]
        for j in range(valid_ref.shape[0]):
            out_ref[j, h] = y + valid_ref[j]


def _prompt_bias_b(tables, tm):
    depth = tables.shape[0]
    w = -(-(PAIR + B_BAND - 1) // LANES) * LANES
    ext = _rel_diagonals(tables, PAIR, -B_BACK * CHUNK, w)
    valid = jnp.asarray(_band_valid(B_BACK, B_BAND, tm) * np.float32(LOG2E))
    nvar = valid.shape[0]
    return pl.pallas_call(
        _rel_bias_kernel,
        grid=(depth,),
        in_specs=[
            pl.BlockSpec((None, B_HEADS, w), lambda l: (l, 0, 0)),
            pl.BlockSpec(valid.shape, lambda l: (0, 0, 0)),
        ],
        out_specs=pl.BlockSpec((None, nvar, B_HEADS, PAIR, B_BAND), lambda l: (l, 0, 0, 0, 0)),
        out_shape=jax.ShapeDtypeStruct((depth, nvar, B_HEADS, PAIR, B_BAND), F32),
        compiler_params=_params(),
        name="rel_bias",
    )(ext, valid)


def _chunk_valid(pos_q, pos_k, back):
    cq, ck = pos_q[:, None] // CHUNK, pos_k[None, :] // CHUNK
    return np.where((ck <= cq) & (ck >= cq - back), 0.0, NEG).astype(np.float32)


def _sample_bias_a(past_len, la, tn):
    pos_q = past_len + np.arange(tn)
    slopes = _alibi_slopes().reshape(A_KV_HEADS, A_GROUP, 1, 1)

    def bias(pos_k):
        dist = np.abs(pos_q[:, None] - pos_k[None, :]).astype(np.float32)
        b = (-slopes * dist + _chunk_valid(pos_q, pos_k, A_BACK)) * np.float32(LOG2E)
        return jnp.asarray(b.reshape(A_KV_HEADS, A_GROUP * tn, pos_k.shape[0]))

    return bias(past_len - la + np.arange(la)), bias(pos_q)


def _sample_bias_b(tables, past_len, lb, tn):
    pos_q = past_len + np.arange(tn)

    def bias(pos_k, offset):
        n_k = pos_k.shape[0]
        ext = _rel_diagonals(tables, tn, offset, n_k + tn - 1)
        rows = [ext[..., tn - 1 - i:tn - 1 - i + n_k] for i in range(tn)]
        valid = _chunk_valid(pos_q, pos_k, B_BACK) * np.float32(LOG2E)
        return jnp.stack(rows, axis=-2) + valid

    return bias(past_len - lb + np.arange(lb), -lb), bias(pos_q, 0)


def _sink_rows(sink, rows):
    s = jnp.repeat(sink.reshape(A_KV_HEADS, A_GROUP, 1) * LOG2E, rows, axis=2)
    return s.reshape(A_KV_HEADS, A_GROUP * rows, 1)


def kernel(x_prompt, x_sample, mem_prompt, cache_a_k, cache_a_v, cache_b_k, cache_b_v, cache_mem_k, cache_mem_v, state_conv, g_mix, w_mix_in, a_sink, b_rel_bias, w_o_a, w_o_b, w_mix_out, g_xattn, g_mem, w_xq, w_xk, w_xv, w_xo, g_ffn, w_up, w_conv, b_conv, w_down, g_final):
    batch, seq, d = x_prompt.shape
    dec_batch, dec_seq, _ = x_sample.shape
    depth = w_mix_in.shape[0]
    n_mem = mem_prompt.shape[1]
    d_ff = w_down.shape[1]
    la, lb = cache_a_k.shape[2], cache_b_k.shape[2]
    past_len = PAST_LEN
    assert batch == 1 and seq % WIDE_TILE == 0 and d_ff % FF_CHUNK == 0
    na, nbk = min(A_BACK * CHUNK, seq), min(B_BACK * CHUNK, seq)
    assert nbk == ROW_TILE and na <= nbk
    ts = dec_batch * dec_seq

    w_in = _prep_w_in(w_mix_in)
    woa = _pair_major(w_o_a, 1).astype(BF16)
    wob = w_o_b.astype(BF16)
    wout = w_mix_out.astype(BF16)
    wxq, wxk, wxv, wxo = (w.astype(BF16) for w in (w_xq, w_xk, w_xv, w_xo))
    wup = w_up.astype(BF16)
    wdown = w_down.astype(BF16)

    rel_tables = b_rel_bias.astype(F32) * LOG2E
    bias_a_p = _prompt_bias_a(ROW_TILE)
    bias_b_p = _prompt_bias_b(rel_tables, ROW_TILE)
    bias_a_sc, bias_a_sn = _sample_bias_a(past_len, la, dec_seq)
    bias_b_sc, bias_b_sn = _sample_bias_b(rel_tables, past_len, lb, dec_seq)
    cak = cache_a_k.reshape(depth, dec_batch, la, A_KV)
    cav = cache_a_v.reshape(depth, dec_batch, la, A_KV)
    cbk = cache_b_k.reshape(depth, dec_batch, lb, B_W)
    cbv = cache_b_v.reshape(depth, dec_batch, lb, B_W)
    cmk = cache_mem_k.reshape(depth, dec_batch, n_mem, X_W)
    cmv = cache_mem_v.reshape(depth, dec_batch, n_mem, X_W)
    zero_state = jnp.zeros((batch, CONV_W - 1, 2 * d_ff), F32)

    pm_k, pm_v = _memkv(mem_prompt[0], g_mem, wxk, wxv)
    pm_k_bf, pm_v_bf = pm_k.astype(BF16), pm_v.astype(BF16)

    xp = x_prompt.reshape(seq, d)
    xs = x_sample.reshape(ts, d)
    outs = {k: [] for k in ("pa_k", "pa_v", "pb_k", "pb_v", "pconv", "sa_k", "sa_v", "sb_k", "sb_v", "sconv")}
    for l in range(depth):
        gfin = g_final if l == depth - 1 else None
        qkv, gate, tail = _inproj(xp, g_mix[l], w_in, l, WIDE_TILE, nbk)
        oa, ob = _attn_prompt(qkv, bias_a_p, bias_b_p, l, _sink_rows(a_sink[l], PAIR))
        xp = _mixout(xp, oa, ob, gate, woa, wob, wout, g_xattn[l], wxq, pm_k_bf, pm_v_bf, l,
                     wxo, WIDE_TILE)
        xp, cp = _ffn(xp.reshape(batch, seq, d), g_ffn[l], zero_state, wup, w_conv[l], b_conv[l],
                      wdown, l, gfin, ROW_TILE)
        xp = xp.reshape(seq, d)
        outs["pb_k"].append(tail[:, 0:B_W].reshape(batch, nbk, B_HEADS, HEAD_DIM))
        outs["pb_v"].append(tail[:, B_W:2 * B_W].reshape(batch, nbk, B_HEADS, HEAD_DIM))
        outs["pa_k"].append(tail[nbk - na:, 2 * B_W:2 * B_W + A_KV].reshape(batch, na, A_KV_HEADS, HEAD_DIM))
        outs["pa_v"].append(tail[nbk - na:, 2 * B_W + A_KV:].reshape(batch, na, A_KV_HEADS, HEAD_DIM))
        outs["pconv"].append(cp)
        qkv, gate, tail = _inproj(xs, g_mix[l], w_in, l, ts, ts)
        oa, ob = _attn_sample(qkv, cak, cav, cbk, cbv, l, dec_seq, bias_a_sc, bias_a_sn,
                              bias_b_sc[l], bias_b_sn[l], _sink_rows(a_sink[l], dec_seq))
        xs = _merge(xs, oa, ob, gate, woa, wob, wout, l, ts)
        xs = _cross(xs.reshape(dec_batch, dec_seq, d), g_xattn[l], wxq, cmk, cmv, wxo, l, dec_seq,
                    lambda b, l=l: (l, b, 0, 0))
        xs, cs = _ffn(xs, g_ffn[l], state_conv[l], wup, w_conv[l], b_conv[l], wdown, l, gfin, dec_seq)
        xs = xs.reshape(ts, d)
        outs["sb_k"].append(tail[:, 0:B_W].reshape(dec_batch, dec_seq, B_HEADS, HEAD_DIM))
        outs["sb_v"].append(tail[:, B_W:2 * B_W].reshape(dec_batch, dec_seq, B_HEADS, HEAD_DIM))
        outs["sa_k"].append(tail[:, 2 * B_W:2 * B_W + A_KV].reshape(dec_batch, dec_seq, A_KV_HEADS, HEAD_DIM))
        outs["sa_v"].append(tail[:, 2 * B_W + A_KV:].reshape(dec_batch, dec_seq, A_KV_HEADS, HEAD_DIM))
        outs["sconv"].append(cs)

    st = {k: jnp.stack(v) for k, v in outs.items()}
    pm_k = pm_k.reshape(depth, batch, n_mem, X_HEADS, X_HEAD_DIM)
    pm_v = pm_v.reshape(depth, batch, n_mem, X_HEADS, X_HEAD_DIM)
    return (xp.reshape(batch, seq, d), xs.reshape(dec_batch, dec_seq, d),
            st["pa_k"], st["pa_v"], st["pb_k"], st["pb_v"], pm_k, pm_v, st["pconv"],
            st["sa_k"], st["sa_v"], st["sb_k"], st["sb_v"], st["sconv"])
```
